```python
import math
import jax, jax.numpy as jnp
from jax import lax
import numpy as np

D_MODEL = 1024
BATCH = 16
SEQ = 2048
DEPTH = 2

N_MEM = 256
N_BRANCH = 5
E_BRANCH = D_MODEL // 2
E_A = E_BRANCH
E_B = E_BRANCH
E_C = E_BRANCH
E_D = E_BRANCH
E_M = E_BRANCH
CONF_WIDTH = 31
H_B = 8
DH_B = E_B // H_B
SB_BLOCK = 128
H_C = 4
DH_C = E_C // H_C
MLSTM_CHUNK = 64
SHORT_CONV = 4
NB_D = 8
BW_D = E_D // NB_D
LRU_C = 8.0
H_M = 4
DH_M = E_M // H_M
EPS = 1e-6
SECTION_SIZES = (E_A, E_A, E_A,
                 E_B, E_B, E_B, E_B,
                 E_C, E_C, E_C, H_C, H_C, E_C, E_C,
                 E_D, E_D,
                 E_M, E_M,
                 N_BRANCH * D_MODEL)
N_IN = sum(SECTION_SIZES)

kernel_name = 'hybrid_gated_parallel_mixers'


def rmsnorm(x, g):
    xf = x.astype(jnp.float32)
    y = xf * lax.rsqrt(jnp.mean(xf * xf, axis=-1, keepdims=True) + EPS)
    return (y * g.astype(jnp.float32)).astype(x.dtype)


def layernorm(x, g, b):
    xf = x.astype(jnp.float32)
    mu = jnp.mean(xf, axis=-1, keepdims=True)
    var = jnp.mean(jnp.square(xf - mu), axis=-1, keepdims=True)
    y = (xf - mu) * lax.rsqrt(var + EPS)
    return (y * g.astype(jnp.float32) + b.astype(jnp.float32)).astype(x.dtype)


def causal_dwconv(x, w, b):
    k = w.shape[0]
    c = x.shape[-1]
    y = lax.conv_general_dilated(x, w.astype(x.dtype)[:, None, :], window_strides=(1,),
                                 padding=((k - 1, 0),),
                                 dimension_numbers=('NWC', 'WIO', 'NWC'),
                                 feature_group_count=c)
    return y + b.astype(x.dtype)


def stick_breaking_attention(q, k, v):
    b, s, h, d = q.shape
    scale = d ** -0.5
    qf, kf, vf = (t.astype(jnp.float32).transpose(0, 2, 1, 3) for t in (q, k, v))
    outs = []
    for blk in range(s // SB_BLOCK):
        q0 = blk * SB_BLOCK
        tk = q0 + SB_BLOCK
        z = jnp.einsum('bhqd,bhkd->bhqk', qf[:, :, q0:tk], kf[:, :, :tk]) * scale
        t_pos = q0 + jnp.arange(SB_BLOCK)[:, None]
        s_pos = jnp.arange(tk)[None, :]
        strict = s_pos < t_pos
        log_keep = jnp.where(strict, jax.nn.log_sigmoid(-z), 0.0)
        later = lax.cumsum(log_keep, axis=3, reverse=True) - log_keep
        w = jnp.where(strict, jnp.exp(jax.nn.log_sigmoid(z) + later), 0.0)
        outs.append(jnp.einsum('bhqk,bhkd->bhqd', w, vf[:, :, :tk]))
    o = jnp.concatenate(outs, axis=2)
    return o.transpose(0, 2, 1, 3).astype(q.dtype)


def mlstm_chunkwise(q, k, v, i_pre, log_f):
    b, s, h, d = q.shape
    L = MLSTM_CHUNK
    nc = s // L

    def chunks(t):
        t = t.astype(jnp.float32).reshape(b, nc, L, h, *t.shape[3:])
        return jnp.moveaxis(t, (1, 3), (0, 2))

    qc, kc, vc = chunks(q), chunks(k) * (d ** -0.5), chunks(v)
    ic, fc = chunks(i_pre), chunks(log_f)
    causal = jnp.tril(jnp.ones((L, L), dtype=bool))

    def step(carry, inp):
        c_mat, n_vec, m_prev = carry
        qb, kb, vb, ib, fb = inp
        bcum = jnp.cumsum(fb, axis=-1)
        d_log = jnp.where(causal, bcum[..., :, None] - bcum[..., None, :] + ib[..., None, :], -jnp.inf)
        inter = bcum + m_prev[..., None]
        m_t = jnp.maximum(inter, jnp.max(d_log, axis=-1))
        w_intra = jnp.exp(d_log - m_t[..., None])
        w_inter = jnp.exp(inter - m_t)
        scores = jnp.einsum('bhtd,bhsd->bhts', qb, kb) * w_intra
        num = (jnp.einsum('bhts,bhsd->bhtd', scores, vb)
               + w_inter[..., None] * jnp.einsum('bhvk,bhtk->bhtv', c_mat, qb))
        den = jnp.sum(scores, axis=-1) + w_inter * jnp.einsum('bhk,bhtk->bht', n_vec, qb)
        h_t = num / jnp.maximum(jnp.abs(den), jnp.exp(-m_t))[..., None]
        b_last = bcum[..., -1]
        w_log = b_last[..., None] - bcum + ib
        m_new = jnp.maximum(b_last + m_prev, jnp.max(w_log, axis=-1))
        w_state = jnp.exp(w_log - m_new[..., None])
        decay = jnp.exp(b_last + m_prev - m_new)
        c_new = decay[..., None, None] * c_mat + jnp.einsum('bhs,bhsv,bhsk->bhvk', w_state, vb, kb)
        n_new = decay[..., None] * n_vec + jnp.einsum('bhs,bhsk->bhk', w_state, kb)
        return (c_new, n_new, m_new), h_t

    init = (jnp.zeros((b, h, d, d), jnp.float32), jnp.zeros((b, h, d), jnp.float32),
            jnp.zeros((b, h), jnp.float32))
    _, hs = lax.scan(step, init, (qc, kc, vc, ic, fc))
    return jnp.moveaxis(hs, (0, 2), (1, 3)).reshape(b, s, h, d)


def rg_lru(x, w_a, b_a, w_x, b_x, lam):
    b, s, e = x.shape
    xf = x.astype(jnp.float32)
    xb = xf.reshape(b, s, NB_D, BW_D)
    r = jax.nn.sigmoid(jnp.einsum('bsnc,ncd->bsnd', xb, w_a.astype(jnp.float32)).reshape(b, s, e) + b_a)
    i = jax.nn.sigmoid(jnp.einsum('bsnc,ncd->bsnd', xb, w_x.astype(jnp.float32)).reshape(b, s, e) + b_x)
    log_a = LRU_C * r * jax.nn.log_sigmoid(lam.astype(jnp.float32))
    a = jnp.exp(log_a)
    u = jnp.sqrt(-jnp.expm1(2.0 * log_a)) * (i * xf)

    def combine(c1, c2):
        return (c1[0] * c2[0], c2[0] * c1[1] + c2[1])

    _, hs = lax.associative_scan(combine, (a, u), axis=1)
    return hs.astype(x.dtype)


def memory_attention(q, mem_n, w_mkv):
    b, s, _ = q.shape
    kv = mem_n @ w_mkv
    mk, mv = jnp.split(kv, 2, axis=-1)
    qh = q.reshape(b, s, H_M, DH_M).astype(jnp.float32)
    kh = mk.reshape(b, -1, H_M, DH_M).astype(jnp.float32)
    vh = mv.reshape(b, -1, H_M, DH_M).astype(jnp.float32)
    p = jax.nn.softmax(jnp.einsum('bshd,bmhd->bhsm', qh, kh) * (DH_M ** -0.5), axis=-1)
    o = jnp.einsum('bhsm,bmhd->bshd', p, vh)
    return o.reshape(b, s, E_M).astype(q.dtype)


def hybrid_layer(x, mem, norm_g, w_in, b_in, a_conv_w, a_conv_b, a_ln_g, a_ln_b,
                 c_conv_w, c_conv_b, c_f_bias, c_hn_g, d_conv_w, d_conv_b,
                 d_wa, d_ba, d_wx, d_bx, d_lambda, mem_norm_g, w_mkv, w_up, w_out):
    b, s, _ = x.shape
    h = rmsnorm(x, norm_g)
    proj = h @ w_in + b_in
    split_points = [int(v) for v in np.cumsum(SECTION_SIZES)[:-1]]
    (a_val, a_glu, a_z,
     b_q, b_k, b_v, b_z,
     c_q, c_k, c_v, c_i, c_f, c_o, c_z,
     d_x, d_z,
     m_q, m_z,
     gate_logits) = jnp.split(proj, split_points, axis=-1)

    u = a_val * jax.nn.sigmoid(a_glu)
    u = layernorm(causal_dwconv(u, a_conv_w, a_conv_b), a_ln_g, a_ln_b)
    y_a = jax.nn.silu(u) * jax.nn.silu(a_z)

    y_b = stick_breaking_attention(b_q.reshape(b, s, H_B, DH_B), b_k.reshape(b, s, H_B, DH_B),
                                   b_v.reshape(b, s, H_B, DH_B)).reshape(b, s, E_B)
    y_b = y_b * jax.nn.silu(b_z)

    qk = jax.nn.silu(causal_dwconv(jnp.concatenate([c_q, c_k], axis=-1), c_conv_w, c_conv_b))
    cq, ck = jnp.split(qk, 2, axis=-1)
    log_f = jax.nn.log_sigmoid((c_f + c_f_bias).astype(jnp.float32))
    hc = mlstm_chunkwise(cq.reshape(b, s, H_C, DH_C), ck.reshape(b, s, H_C, DH_C),
                         c_v.reshape(b, s, H_C, DH_C), c_i, log_f)
    mu = jnp.mean(hc, axis=-1, keepdims=True)
    var = jnp.mean(jnp.square(hc - mu), axis=-1, keepdims=True)
    hc = ((hc - mu) * lax.rsqrt(var + EPS)).reshape(b, s, E_C) * c_hn_g.astype(jnp.float32)
    y_c = hc.astype(x.dtype) * jax.nn.sigmoid(c_o) * jax.nn.silu(c_z)

    y_d = rg_lru(causal_dwconv(d_x, d_conv_w, d_conv_b), d_wa, d_ba, d_wx, d_bx, d_lambda)
    y_d = y_d * jax.nn.silu(d_z)

    y_m = memory_attention(m_q, rmsnorm(mem, mem_norm_g), w_mkv) * jax.nn.silu(m_z)

    gates = jax.nn.sigmoid(gate_logits).reshape(b, s, N_BRANCH, D_MODEL)
    merged = jnp.zeros_like(x)
    for n, y in enumerate((y_a, y_b, y_c, y_d, y_m)):
        merged = merged + gates[:, :, n] * (y @ w_up[n])
    return x + merged @ w_out


def setup_inputs(seed: int = 0) -> dict:
    key = jax.random.key(seed)
    ks = jax.random.split(key, 32)
    f32 = jnp.float32

    def nrm(k, shape, scale):
        return jax.random.normal(k, shape, f32) * scale

    u = jax.random.uniform(ks[20], (DEPTH, E_D), f32, minval=0.9, maxval=0.999)
    p = u ** (1.0 / LRU_C)
    d_lambda = jnp.log(p) - jnp.log1p(-p)
    return {
        'x': nrm(ks[0], (BATCH, SEQ, D_MODEL), 1.0),
        'mem': nrm(ks[1], (BATCH, N_MEM, D_MODEL), 1.0),
        'norm_g': 1.0 + nrm(ks[2], (DEPTH, D_MODEL), 0.02),
        'w_in': nrm(ks[3], (DEPTH, D_MODEL, N_IN), D_MODEL ** -0.5),
        'b_in': nrm(ks[4], (DEPTH, N_IN), 0.02),
        'a_conv_w': nrm(ks[5], (DEPTH, CONF_WIDTH, E_A), CONF_WIDTH ** -0.5),
        'a_conv_b': nrm(ks[6], (DEPTH, E_A), 0.02),
        'a_ln_g': 1.0 + nrm(ks[7], (DEPTH, E_A), 0.02),
        'a_ln_b': nrm(ks[8], (DEPTH, E_A), 0.02),
        'c_conv_w': nrm(ks[9], (DEPTH, SHORT_CONV, 2 * E_C), SHORT_CONV ** -0.5),
        'c_conv_b': nrm(ks[10], (DEPTH, 2 * E_C), 0.02),
        'c_f_bias': jnp.linspace(3.0, 6.0, H_C, dtype=f32)[None, :] + nrm(ks[11], (DEPTH, H_C), 0.1),
        'c_hn_g': 1.0 + nrm(ks[12], (DEPTH, E_C), 0.02),
        'd_conv_w': nrm(ks[13], (DEPTH, SHORT_CONV, E_D), SHORT_CONV ** -0.5),
        'd_conv_b': nrm(ks[14], (DEPTH, E_D), 0.02),
        'd_wa': nrm(ks[15], (DEPTH, NB_D, BW_D, BW_D), BW_D ** -0.5),
        'd_ba': nrm(ks[16], (DEPTH, E_D), 0.02),
        'd_wx': nrm(ks[17], (DEPTH, NB_D, BW_D, BW_D), BW_D ** -0.5),
        'd_bx': nrm(ks[18], (DEPTH, E_D), 0.02),
        'd_lambda': d_lambda,
        'mem_norm_g': 1.0 + nrm(ks[21], (DEPTH, D_MODEL), 0.02),
        'w_mkv': nrm(ks[22], (DEPTH, D_MODEL, 2 * E_M), D_MODEL ** -0.5),
        'w_up': nrm(ks[23], (DEPTH, N_BRANCH, E_BRANCH, D_MODEL), E_BRANCH ** -0.5),
        'w_out': nrm(ks[24], (DEPTH, D_MODEL, D_MODEL), D_MODEL ** -0.5),
        'final_norm_g': 1.0 + nrm(ks[25], (D_MODEL,), 0.02),
    }


def reference(x, mem, norm_g, w_in, b_in, a_conv_w, a_conv_b, a_ln_g, a_ln_b,
              c_conv_w, c_conv_b, c_f_bias, c_hn_g, d_conv_w, d_conv_b,
              d_wa, d_ba, d_wx, d_bx, d_lambda, mem_norm_g, w_mkv, w_up, w_out,
              final_norm_g):
    for l in range(DEPTH):
        x = hybrid_layer(x, mem, norm_g[l], w_in[l], b_in[l], a_conv_w[l], a_conv_b[l],
                         a_ln_g[l], a_ln_b[l], c_conv_w[l], c_conv_b[l], c_f_bias[l], c_hn_g[l],
                         d_conv_w[l], d_conv_b[l], d_wa[l], d_ba[l], d_wx[l], d_bx[l], d_lambda[l],
                         mem_norm_g[l], w_mkv[l], w_up[l], w_out[l])
    return rmsnorm(x, final_norm_g)
```

```python
import functools

import jax
import jax.numpy as jnp
from jax import lax
from jax.experimental import pallas as pl
from jax.experimental.pallas import tpu as pltpu

F32 = jnp.float32
BF16 = jnp.bfloat16

D_MODEL = 1024
E_BRANCH = 512
N_BRANCH = 5
CONF_WIDTH = 31
H_B = 8
DH_B = E_BRANCH // H_B
H_C = 4
DH_C = E_BRANCH // H_C
SHORT_CONV = 4
NB_D = 8
BW_D = E_BRANCH // NB_D
LRU_C = 8.0
H_M = 4
DH_M = E_BRANCH // H_M
EPS = 1e-6

LANES = 128
SUBLANES = 8
VMEM_LIMIT = 56 * 1024 * 1024

COL_A_VAL, COL_A_GLU, COL_A_Z = 0, 1, 2
COL_B_Q, COL_B_K, COL_B_V, COL_B_Z = 3, 4, 5, 6
COL_C_Q, COL_C_K, COL_C_V, COL_C_O, COL_C_Z = 7, 8, 9, 10, 11
COL_D_X, COL_D_Z = 12, 13
COL_M_Q, COL_M_Z = 14, 15
N_MAIN = 16 * E_BRANCH
IF_WIDTH = 2 * LANES

TM_IN, TN_IN = 1024, 2048
T_CONV_A = 256
ROWS_CONV_A = 32
HALO_A = 32
BLK = 128
TQ_MEM = 512
TM_MERGE = 512
ROWS_D = 256


def _dot(a, b):
    return jnp.dot(a, b, preferred_element_type=F32)


def _dot_nt(a, b):
    return lax.dot_general(a, b, (((1,), (1,)), ((), ())), preferred_element_type=F32)


def _sigmoid(x):
    return 0.5 * jnp.tanh(0.5 * x) + 0.5


def _silu(x):
    return x * _sigmoid(x)


def _neg_softplus(x):
    return -(jnp.maximum(x, 0.0) + jnp.log(1.0 + jnp.exp(-jnp.abs(x))))


def _split_bf16(x, parts):
    out = []
    r = x
    for _ in range(parts):
        p = r.astype(BF16)
        out.append(p)
        r = r - p.astype(F32)
    return out


def _params(*sem):
    return pltpu.CompilerParams(dimension_semantics=sem, vmem_limit_bytes=VMEM_LIMIT)


def _const_spec(shape):
    nd = len(shape)
    return pl.BlockSpec(shape, lambda *_: (0,) * nd, pipeline_mode=pl.Buffered(1))


def _inproj_kernel(x_ref, g_ref, w_ref, b_ref, wif_ref, bif_ref, proj_ref, h_ref, if_ref, hs_ref):
    @pl.when(pl.program_id(1) == 0)
    def _():
        xf = x_ref[...]
        ms = jnp.mean(xf * xf, axis=-1, keepdims=True)
        hb = (xf * lax.rsqrt(ms + EPS) * g_ref[...]).astype(BF16)
        hs_ref[...] = hb
        h_ref[...] = hb
        if_ref[...] = _dot(hb, wif_ref[...]) + bif_ref[...]

    proj_ref[...] = (_dot(hs_ref[...], w_ref[...]) + b_ref[...]).astype(proj_ref.dtype)


def _inproj(x2, g, w_main, b_main, wif, bif):
    t = x2.shape[0]
    return pl.pallas_call(
        _inproj_kernel,
        grid=(t // TM_IN, N_MAIN // TN_IN),
        in_specs=[
            pl.BlockSpec((TM_IN, D_MODEL), lambda m, n: (m, 0)),
            _const_spec((1, D_MODEL)),
            pl.BlockSpec((D_MODEL, TN_IN), lambda m, n: (0, n)),
            pl.BlockSpec((1, TN_IN), lambda m, n: (0, n)),
            _const_spec((D_MODEL, IF_WIDTH)),
            _const_spec((1, IF_WIDTH)),
        ],
        out_specs=[
            pl.BlockSpec((TM_IN, TN_IN), lambda m, n: (m, n)),
            pl.BlockSpec((TM_IN, D_MODEL), lambda m, n: (m, 0)),
            pl.BlockSpec((TM_IN, IF_WIDTH), lambda m, n: (m, 0)),
        ],
        out_shape=[
            jax.ShapeDtypeStruct((t, N_MAIN), BF16),
            jax.ShapeDtypeStruct((t, D_MODEL), BF16),
            jax.ShapeDtypeStruct((t, IF_WIDTH), F32),
        ],
        scratch_shapes=[pltpu.VMEM((TM_IN, D_MODEL), BF16)],
        compiler_params=_params("parallel", "arbitrary"),
        name="inproj",
    )(x2, g, w_main, b_main, wif, bif)


def _conv_a_kernel(val_ref, glu_ref, z_ref, cw_ref, cb_ref, lg_ref, lb_ref, o_ref, upad):
    s = pl.program_id(1)

    @pl.when(s == 0)
    def _():
        upad[0:HALO_A, :] = jnp.zeros((HALO_A, E_BRANCH), F32)

    @pl.when(s > 0)
    def _():
        upad[0:HALO_A, :] = upad[T_CONV_A:T_CONV_A + HALO_A, :]

    upad[HALO_A:HALO_A + T_CONV_A, :] = val_ref[...].astype(F32) * _sigmoid(glu_ref[...].astype(F32))

    first = HALO_A - (CONF_WIDTH - 1)
    for c in range(T_CONV_A // ROWS_CONV_A):
        r0 = c * ROWS_CONV_A
        acc = jnp.broadcast_to(cb_ref[...], (ROWS_CONV_A, E_BRANCH))
        for j in range(CONF_WIDTH):
            acc = acc + cw_ref[j:j + 1, :] * upad[r0 + first + j:r0 + first + j + ROWS_CONV_A, :]
        mu = jnp.mean(acc, axis=-1, keepdims=True)
        d = acc - mu
        var = jnp.mean(d * d, axis=-1, keepdims=True)
        y = d * lax.rsqrt(var + EPS) * lg_ref[...] + lb_ref[...]
        zz = z_ref[r0:r0 + ROWS_CONV_A, :].astype(F32)
        o_ref[r0:r0 + ROWS_CONV_A, :] = (_silu(y) * _silu(zz)).astype(o_ref.dtype)


def _conv_a(proj, batch, seq, cw, cb, lg, lb):
    ns = seq // T_CONV_A

    def col(c):
        return pl.BlockSpec((T_CONV_A, E_BRANCH), lambda b, s: (b * ns + s, c))

    return pl.pallas_call(
        _conv_a_kernel,
        grid=(batch, ns),
        in_specs=[col(COL_A_VAL), col(COL_A_GLU), col(COL_A_Z),
                  _const_spec((CONF_WIDTH, E_BRANCH)), _const_spec((1, E_BRANCH)),
                  _const_spec((1, E_BRANCH)), _const_spec((1, E_BRANCH))],
        out_specs=pl.BlockSpec((T_CONV_A, E_BRANCH), lambda b, s: (b * ns + s, 0)),
        out_shape=jax.ShapeDtypeStruct((batch * seq, E_BRANCH), BF16),
        scratch_shapes=[pltpu.VMEM((HALO_A + T_CONV_A, E_BRANCH), F32)],
        compiler_params=_params("parallel", "arbitrary"),
        name="conv_a",
    )(proj, proj, proj, cw, cb, lg, lb)


def _sb_kernel(q_ref, k_ref, v_ref, z_ref, u_ref, o_ref, qm_ref, acc_ref, r_ref):
    qi = pl.program_id(1)
    lane = lax.broadcasted_iota(jnp.int32, (BLK, LANES), 1)
    row = lax.broadcasted_iota(jnp.int32, (BLK, LANES), 0)
    strict = lane < row
    lo_half = lane < DH_B
    scale = DH_B ** -0.5

    for h in range(H_B):
        p = h // 2
        qp = q_ref[:, p * LANES:(p + 1) * LANES].astype(F32) * scale
        keep = lo_half if h % 2 == 0 else jnp.logical_not(lo_half)
        qm_ref[h] = jnp.where(keep, qp, 0.0).astype(BF16)

    def head_step(h, kb, diag):
        p = h // 2
        start = pl.multiple_of(kb * BLK, BLK)
        kp = k_ref[pl.ds(start, BLK), p * LANES:(p + 1) * LANES]
        vp = v_ref[pl.ds(start, BLK), p * LANES:(p + 1) * LANES]
        z = _dot_nt(qm_ref[h], kp)
        lk = _neg_softplus(z)
        if diag:
            lk = jnp.where(strict, lk, 0.0)
        hi, lo = _split_bf16(lk, 2)
        cs = _dot(hi, u_ref[...]) + _dot(lo, u_ref[...])
        cum = cs[:, :LANES]
        tot = cs[:, LANES:]
        if diag:
            w = jnp.where(strict, jnp.exp(z + cum), 0.0)
            acc_ref[h] = _dot(w.astype(BF16), vp)
            r_ref[h] = tot
        else:
            w = jnp.exp(z + cum + r_ref[h])
            acc_ref[h] += _dot(w.astype(BF16), vp)
            r_ref[h] += tot

    for h in range(H_B):
        head_step(h, qi, True)

    def body(i, carry):
        kb = qi - 1 - i
        for h in range(H_B):
            head_step(h, kb, False)
        return carry

    lax.fori_loop(0, qi, body, 0)

    for p in range(H_B // 2):
        o_pair = jnp.where(lo_half, acc_ref[2 * p], acc_ref[2 * p + 1])
        zz = z_ref[:, p * LANES:(p + 1) * LANES].astype(F32)
        o_ref[:, p * LANES:(p + 1) * LANES] = (o_pair * _silu(zz)).astype(o_ref.dtype)


def _sb_attn(proj, batch, seq, umat):
    nq = seq // BLK
    return pl.pallas_call(
        _sb_kernel,
        grid=(batch, nq),
        in_specs=[
            pl.BlockSpec((BLK, E_BRANCH), lambda b, q: (b * nq + q, COL_B_Q)),
            pl.BlockSpec((seq, E_BRANCH), lambda b, q: (b, COL_B_K)),
            pl.BlockSpec((seq, E_BRANCH), lambda b, q: (b, COL_B_V)),
            pl.BlockSpec((BLK, E_BRANCH), lambda b, q: (b * nq + q, COL_B_Z)),
            _const_spec((BLK, 2 * LANES)),
        ],
        out_specs=pl.BlockSpec((BLK, E_BRANCH), lambda b, q: (b * nq + q, 0)),
        out_shape=jax.ShapeDtypeStruct((batch * seq, E_BRANCH), BF16),
        scratch_shapes=[pltpu.VMEM((H_B, BLK, LANES), BF16),
                        pltpu.VMEM((H_B, BLK, LANES), F32),
                        pltpu.VMEM((H_B, BLK, LANES), F32)],
        compiler_params=_params("parallel", "arbitrary"),
        name="sb_attn",
    )(proj, proj, proj, proj, umat)


def _mlstm_kernel(q_ref, k_ref, v_ref, og_ref, z_ref, ifc_ref, ir_ref, fr_ref,
                  wq_ref, wk_ref, bq_ref, bk_ref, fbc_ref, fbr_ref, g_ref, lt_ref, ut_ref,
                  y_ref, qpad, kpad, ct_ref, m_ref):
    c = pl.program_id(1)
    halo = SUBLANES

    @pl.when(c == 0)
    def _():
        qpad[0:halo, :] = jnp.zeros((halo, E_BRANCH), F32)
        kpad[0:halo, :] = jnp.zeros((halo, E_BRANCH), F32)
        ct_ref[...] = jnp.zeros(ct_ref.shape, F32)
        m_ref[...] = jnp.zeros(m_ref.shape, F32)

    @pl.when(c > 0)
    def _():
        qpad[0:halo, :] = qpad[BLK:BLK + halo, :]
        kpad[0:halo, :] = kpad[BLK:BLK + halo, :]

    qpad[halo:halo + BLK, :] = q_ref[...].astype(F32)
    kpad[halo:halo + BLK, :] = k_ref[...].astype(F32)

    first = halo - (SHORT_CONV - 1)
    qc = jnp.broadcast_to(bq_ref[...], (BLK, E_BRANCH))
    kc = jnp.broadcast_to(bk_ref[...], (BLK, E_BRANCH))
    for j in range(SHORT_CONV):
        qc = qc + wq_ref[j:j + 1, :] * qpad[first + j:first + j + BLK, :]
        kc = kc + wk_ref[j:j + 1, :] * kpad[first + j:first + j + BLK, :]
    qc = _silu(qc).astype(BF16)
    kc = _silu(kc) * (DH_C ** -0.5)

    ic = ifc_ref[:, 0:LANES]
    fc = _neg_softplus(-(ifc_ref[:, LANES:2 * LANES] + fbc_ref[...]))
    bc = sum(_dot(lt_ref[...], part) for part in _split_bf16(fc, 3))
    ir = ir_ref[0]
    fr = _neg_softplus(-(fr_ref[0] + fbr_ref[...]))
    br = sum(_dot(part, ut_ref[...]) for part in _split_bf16(fr, 3))

    m_prev = m_ref[...]
    b_last = bc[BLK - 1:BLK, :]
    wlog = b_last - bc + ic
    m_new = jnp.maximum(b_last + m_prev, jnp.max(wlog, axis=0, keepdims=True))
    wst = jnp.exp(wlog - m_new)
    decay = jnp.exp(b_last + m_prev - m_new)
    inter = bc + m_prev

    lane = lax.broadcasted_iota(jnp.int32, (BLK, LANES), 1)
    row = lax.broadcasted_iota(jnp.int32, (BLK, LANES), 0)
    causal = lane <= row

    for h in range(H_C):
        sl = slice(h * LANES, (h + 1) * LANES)
        qh = qc[:, sl]
        kh = kc[:, sl]
        vh = v_ref[:, sl]
        s_qk = _dot_nt(qh, kh.astype(BF16))
        dlog = jnp.where(causal, bc[:, h:h + 1] - br[h:h + 1, :] + ir[h:h + 1, :], -jnp.inf)
        inter_h = inter[:, h:h + 1]
        mt = jnp.maximum(inter_h, jnp.max(dlog, axis=1, keepdims=True))
        w_intra = jnp.exp(dlog - mt)
        w_inter = jnp.exp(inter_h - mt)
        sc = s_qk * w_intra
        qct = _dot(qh, ct_ref[h].astype(BF16))
        num = _dot(sc.astype(BF16), vh) + w_inter * qct[:, :LANES]
        den = jnp.sum(sc, axis=1, keepdims=True) + w_inter * qct[:, LANES:LANES + 1]
        hh = num / jnp.maximum(jnp.abs(den), jnp.exp(-mt))
        mu = jnp.mean(hh, axis=-1, keepdims=True)
        dd = hh - mu
        var = jnp.mean(dd * dd, axis=-1, keepdims=True)
        hn = dd * lax.rsqrt(var + EPS) * g_ref[:, sl]
        y = hn * _sigmoid(og_ref[:, sl].astype(F32)) * _silu(z_ref[:, sl].astype(F32))
        y_ref[:, sl] = y.astype(y_ref.dtype)

        w_col = wst[:, h:h + 1]
        vw = jnp.concatenate([vh.astype(F32) * w_col, jnp.where(lane == 0, w_col, 0.0)], axis=1)
        k_t = kh.T.astype(BF16)
        ct_ref[h] = decay[:, h:h + 1] * ct_ref[h] + _dot(k_t, vw.astype(BF16))

    m_ref[...] = m_new


def _mlstm(proj, ifv, irow, frow, batch, seq, wq, wk, bq, bk, fbc, fbr, hn_g, ltri, utri):
    nc = seq // BLK

    def col(c):
        return pl.BlockSpec((BLK, E_BRANCH), lambda b, s: (b * nc + s, c))

    row_spec = pl.BlockSpec((1, SUBLANES, BLK), lambda b, s: (b, 0, s))
    return pl.pallas_call(
        _mlstm_kernel,
        grid=(batch, nc),
        in_specs=[col(COL_C_Q), col(COL_C_K), col(COL_C_V), col(COL_C_O), col(COL_C_Z),
                  pl.BlockSpec((BLK, IF_WIDTH), lambda b, s: (b * nc + s, 0)),
                  row_spec, row_spec,
                  _const_spec((SHORT_CONV, E_BRANCH)), _const_spec((SHORT_CONV, E_BRANCH)),
                  _const_spec((1, E_BRANCH)), _const_spec((1, E_BRANCH)),
                  _const_spec((1, LANES)), _const_spec((SUBLANES, LANES)),
                  _const_spec((1, E_BRANCH)),
                  _const_spec((BLK, BLK)), _const_spec((BLK, BLK))],
        out_specs=pl.BlockSpec((BLK, E_BRANCH), lambda b, s: (b * nc + s, 0)),
        out_shape=jax.ShapeDtypeStruct((batch * seq, E_BRANCH), BF16),
        scratch_shapes=[pltpu.VMEM((SUBLANES + BLK, E_BRANCH), F32),
                        pltpu.VMEM((SUBLANES + BLK, E_BRANCH), F32),
                        pltpu.VMEM((H_C, DH_C, 2 * LANES), F32),
                        pltpu.VMEM((1, LANES), F32)],
        compiler_params=_params("parallel", "arbitrary"),
        name="mlstm",
    )(proj, proj, proj, proj, proj, ifv, irow, frow, wq, wk, bq, bk, fbc, fbr, hn_g, ltri, utri)


def _rglru_kernel(x_ref, z_ref, cw_ref, cb_ref, wa_ref, ba_ref, wx_ref, bx_ref, lam_ref,
                  y_ref, xpad, a_s, h_s):
    seq = x_ref.shape[0]
    halo = SUBLANES
    n_grp = E_BRANCH // LANES
    xpad[0:halo, :] = jnp.zeros((halo, E_BRANCH), F32)
    xpad[halo:halo + seq, :] = x_ref[...].astype(F32)
    log_lam = _neg_softplus(-lam_ref[...])

    first = halo - (SHORT_CONV - 1)
    for c in range(seq // ROWS_D):
        r0 = c * ROWS_D
        xc = jnp.broadcast_to(cb_ref[...], (ROWS_D, E_BRANCH))
        for j in range(SHORT_CONV):
            xc = xc + cw_ref[j:j + 1, :] * xpad[r0 + first + j:r0 + first + j + ROWS_D, :]
        xb = xc.astype(BF16)
        r = _sigmoid(_dot(xb, wa_ref[...]) + ba_ref[...])
        i = _sigmoid(_dot(xb, wx_ref[...]) + bx_ref[...])
        log_a = LRU_C * r * log_lam
        a = jnp.exp(log_a)
        u = jnp.sqrt(-jnp.tanh(log_a) * (a * a + 1.0)) * (i * xc)
        for g in range(n_grp):
            a_s[g, r0:r0 + ROWS_D, :] = a[:, g * LANES:(g + 1) * LANES]
            h_s[g, r0:r0 + ROWS_D, :] = u[:, g * LANES:(g + 1) * LANES]

    seg = seq // SUBLANES

    def scan_body(t, carry):
        idx = pl.ds(t, SUBLANES, stride=seg)
        out = []
        for g in range(n_grp):
            h, ap = carry[g]
            a8 = a_s[g, idx, :]
            h = a8 * h + h_s[g, idx, :]
            ap = ap * a8
            h_s[g, idx, :] = h
            a_s[g, idx, :] = ap
            out.append((h, ap))
        return tuple(out)

    init = tuple((jnp.zeros((SUBLANES, LANES), F32), jnp.ones((SUBLANES, LANES), F32))
                 for _ in range(n_grp))
    ends = lax.fori_loop(0, seg, scan_body, init)
    carries = []
    for h_end, a_end in ends:
        carry = jnp.zeros((1, LANES), F32)
        rows = []
        for j in range(SUBLANES):
            rows.append(carry)
            carry = h_end[j:j + 1, :] + a_end[j:j + 1, :] * carry
        carries.append(jnp.concatenate(rows, axis=0))

    def fix_body(t, _):
        idx = pl.ds(t, SUBLANES, stride=seg)
        for g in range(n_grp):
            h_s[g, idx, :] = h_s[g, idx, :] + a_s[g, idx, :] * carries[g]
        return 0

    lax.fori_loop(0, seg, fix_body, 0)

    for c in range(seq // ROWS_D):
        r0 = c * ROWS_D
        hs = jnp.concatenate([h_s[g, r0:r0 + ROWS_D, :] for g in range(n_grp)], axis=1)
        zz = z_ref[r0:r0 + ROWS_D, :].astype(F32)
        y_ref[r0:r0 + ROWS_D, :] = (hs * _silu(zz)).astype(y_ref.dtype)


def _rglru(proj, batch, seq, cw, cb, wa, ba, wx, bx, lam):
    return pl.pallas_call(
        _rglru_kernel,
        grid=(batch,),
        in_specs=[pl.BlockSpec((seq, E_BRANCH), lambda b: (b, COL_D_X)),
                  pl.BlockSpec((seq, E_BRANCH), lambda b: (b, COL_D_Z)),
                  _const_spec((SHORT_CONV, E_BRANCH)), _const_spec((1, E_BRANCH)),
                  _const_spec((E_BRANCH, E_BRANCH)), _const_spec((1, E_BRANCH)),
                  _const_spec((E_BRANCH, E_BRANCH)), _const_spec((1, E_BRANCH)),
                  _const_spec((1, E_BRANCH))],
        out_specs=pl.BlockSpec((seq, E_BRANCH), lambda b: (b, 0)),
        out_shape=jax.ShapeDtypeStruct((batch * seq, E_BRANCH), BF16),
        scratch_shapes=[pltpu.VMEM((SUBLANES + seq, E_BRANCH), F32),
                        pltpu.VMEM((E_BRANCH // LANES, seq, LANES), F32),
                        pltpu.VMEM((E_BRANCH // LANES, seq, LANES), F32)],
        compiler_params=_params("parallel"),
        name="rglru",
    )(proj, proj, cw, cb, wa, ba, wx, bx, lam)


def _memkv_kernel(mem_ref, g_ref, w_ref, kv_ref):
    xf = mem_ref[...]
    ms = jnp.mean(xf * xf, axis=-1, keepdims=True)
    mn = (xf * lax.rsqrt(ms + EPS) * g_ref[...]).astype(BF16)
    kv_ref[...] = _dot(mn, w_ref[...]).astype(kv_ref.dtype)


def _memkv(mem2, batch, n_mem, g, w):
    return pl.pallas_call(
        _memkv_kernel,
        grid=(batch,),
        in_specs=[pl.BlockSpec((n_mem, D_MODEL), lambda b: (b, 0)),
                  _const_spec((1, D_MODEL)),
                  _const_spec((D_MODEL, 2 * E_BRANCH))],
        out_specs=pl.BlockSpec((n_mem, 2 * E_BRANCH), lambda b: (b, 0)),
        out_shape=jax.ShapeDtypeStruct((batch * n_mem, 2 * E_BRANCH), BF16),
        compiler_params=_params("parallel"),
        name="memkv",
    )(mem2, g, w)


def _memattn_kernel(q_ref, z_ref, kv_ref, y_ref):
    scale = DH_M ** -0.5
    for h in range(H_M):
        sl = slice(h * LANES, (h + 1) * LANES)
        kh = kv_ref[:, sl]
        vh = kv_ref[:, E_BRANCH + h * LANES:E_BRANCH + (h + 1) * LANES]
        s = _dot_nt(q_ref[:, sl], kh) * scale
        p = jnp.exp(s - jnp.max(s, axis=-1, keepdims=True))
        o = _dot(p.astype(BF16), vh) / jnp.sum(p, axis=-1, keepdims=True)
        y_ref[:, sl] = (o * _silu(z_ref[:, sl].astype(F32))).astype(y_ref.dtype)


def _memattn(proj, kv, batch, seq, n_mem):
    nq = seq // TQ_MEM
    return pl.pallas_call(
        _memattn_kernel,
        grid=(batch, nq),
        in_specs=[pl.BlockSpec((TQ_MEM, E_BRANCH), lambda b, q: (b * nq + q, COL_M_Q)),
                  pl.BlockSpec((TQ_MEM, E_BRANCH), lambda b, q: (b * nq + q, COL_M_Z)),
                  pl.BlockSpec((n_mem, 2 * E_BRANCH), lambda b, q: (b, 0))],
        out_specs=pl.BlockSpec((TQ_MEM, E_BRANCH), lambda b, q: (b * nq + q, 0)),
        out_shape=jax.ShapeDtypeStruct((batch * seq, E_BRANCH), BF16),
        compiler_params=_params("parallel", "parallel"),
        name="memattn",
    )(proj, proj, kv)


def _merge_kernel(x_ref, h_ref, ya_ref, yb_ref, yc_ref, yd_ref, ym_ref, wg_ref, bg_ref,
                  wup_ref, wout_ref, fg_ref, o_ref, *, final_norm):
    hb = h_ref[...]
    merged = None
    for n, y_ref in enumerate((ya_ref, yb_ref, yc_ref, yd_ref, ym_ref)):
        cols = slice(n * D_MODEL, (n + 1) * D_MODEL)
        gate = _sigmoid(_dot(hb, wg_ref[:, cols]) + bg_ref[:, cols])
        term = gate * _dot(y_ref[...], wup_ref[n])
        merged = term if merged is None else merged + term
    out = x_ref[...] + _dot(merged.astype(BF16), wout_ref[...])
    if final_norm:
        ms = jnp.mean(out * out, axis=-1, keepdims=True)
        out = out * lax.rsqrt(ms + EPS) * fg_ref[...]
    o_ref[...] = out


def _merge(x2, h, ys, wg, bg, wup, wout, fg, final_norm):
    t = x2.shape[0]
    row = lambda width: pl.BlockSpec((TM_MERGE, width), lambda m: (m, 0))
    return pl.pallas_call(
        functools.partial(_merge_kernel, final_norm=final_norm),
        grid=(t // TM_MERGE,),
        in_specs=[row(D_MODEL), row(D_MODEL)] + [row(E_BRANCH)] * N_BRANCH + [
            _const_spec((D_MODEL, N_BRANCH * D_MODEL)), _const_spec((1, N_BRANCH * D_MODEL)),
            _const_spec((N_BRANCH, E_BRANCH, D_MODEL)), _const_spec((D_MODEL, D_MODEL)),
            _const_spec((1, D_MODEL))],
        out_specs=row(D_MODEL),
        out_shape=jax.ShapeDtypeStruct((t, D_MODEL), F32),
        compiler_params=_params("parallel"),
        name="merge_final" if final_norm else "merge",
    )(x2, h, *ys, wg, bg, wup, wout, fg)


def _block_diag(w):
    nb, bw, _ = w.shape
    eye = jnp.eye(nb, dtype=w.dtype)
    return (eye[:, None, :, None] * w[:, :, None, :]).reshape(nb * bw, nb * bw)


def _layer(x2, mem2, batch, seq, n_mem, norm_g, w_in, b_in, a_conv_w, a_conv_b, a_ln_g, a_ln_b,
           c_conv_w, c_conv_b, c_f_bias, c_hn_g, d_conv_w, d_conv_b, d_wa, d_ba, d_wx, d_bx,
           d_lambda, mem_norm_g, w_mkv, w_up, w_out, final_g, final_norm, consts):
    umat, ltri, utri = consts
    row = lambda v: v.reshape(1, -1)
    n_pre = 10 * E_BRANCH
    n_if = 2 * H_C
    n_gate0 = N_MAIN + n_if
    w_main = jnp.concatenate([w_in[:, :n_pre], w_in[:, n_pre + n_if:n_gate0]], axis=1).astype(BF16)
    b_main = row(jnp.concatenate([b_in[:n_pre], b_in[n_pre + n_if:n_gate0]]))
    wif = jnp.zeros((D_MODEL, IF_WIDTH), F32)
    wif = wif.at[:, 0:H_C].set(w_in[:, n_pre:n_pre + H_C])
    wif = wif.at[:, LANES:LANES + H_C].set(w_in[:, n_pre + H_C:n_pre + n_if]).astype(BF16)
    bif = jnp.zeros((1, IF_WIDTH), F32)
    bif = bif.at[0, 0:H_C].set(b_in[n_pre:n_pre + H_C])
    bif = bif.at[0, LANES:LANES + H_C].set(b_in[n_pre + H_C:n_pre + n_if])

    proj, h, ifv = _inproj(x2, row(norm_g), w_main, b_main, wif, bif)

    y_a = _conv_a(proj, batch, seq, a_conv_w, row(a_conv_b), row(a_ln_g), row(a_ln_b))
    y_b = _sb_attn(proj, batch, seq, umat)

    pad_rows = ((0, 0), (0, SUBLANES - H_C), (0, 0))
    irow = jnp.pad(ifv[:, 0:H_C].reshape(batch, seq, H_C).transpose(0, 2, 1), pad_rows)
    frow = jnp.pad(ifv[:, LANES:LANES + H_C].reshape(batch, seq, H_C).transpose(0, 2, 1), pad_rows)
    fbc = jnp.zeros((1, LANES), F32).at[0, 0:H_C].set(c_f_bias)
    fbr = jnp.zeros((SUBLANES, LANES), F32).at[0:H_C, :].set(
        jnp.broadcast_to(c_f_bias[:, None], (H_C, LANES)))
    y_c = _mlstm(proj, ifv, irow, frow, batch, seq,
                 c_conv_w[:, :E_BRANCH], c_conv_w[:, E_BRANCH:],
                 row(c_conv_b[:E_BRANCH]), row(c_conv_b[E_BRANCH:]),
                 fbc, fbr, row(c_hn_g), ltri, utri)

    y_d = _rglru(proj, batch, seq, d_conv_w, row(d_conv_b),
                 _block_diag(d_wa).astype(BF16), row(d_ba),
                 _block_diag(d_wx).astype(BF16), row(d_bx), row(d_lambda))

    kv = _memkv(mem2, batch, n_mem, row(mem_norm_g), w_mkv.astype(BF16))
    y_m = _memattn(proj, kv, batch, seq, n_mem)

    return _merge(x2, h, (y_a, y_b, y_c, y_d, y_m), w_in[:, n_gate0:].astype(BF16),
                  row(b_in[n_gate0:]), w_up.astype(BF16), w_out.astype(BF16), row(final_g),
                  final_norm)


def kernel(x, mem, norm_g, w_in, b_in, a_conv_w, a_conv_b, a_ln_g, a_ln_b, c_conv_w, c_conv_b,
           c_f_bias, c_hn_g, d_conv_w, d_conv_b, d_wa, d_ba, d_wx, d_bx, d_lambda, mem_norm_g,
           w_mkv, w_up, w_out, final_norm_g):
    batch, seq, _ = x.shape
    n_mem = mem.shape[1]
    depth = w_in.shape[0]
    assert seq % T_CONV_A == 0 and seq % TQ_MEM == 0 and seq % ROWS_D == 0
    assert (batch * seq) % TM_IN == 0 and (batch * seq) % TM_MERGE == 0

    tri = jnp.tril(jnp.ones((BLK, BLK), F32))
    umat = jnp.concatenate([tri, jnp.ones((BLK, LANES), F32)], axis=1).astype(BF16)
    consts = (umat, tri.astype(BF16), tri.T.astype(BF16))

    x2 = x.reshape(batch * seq, D_MODEL)
    mem2 = mem.reshape(batch * n_mem, D_MODEL)
    for l in range(depth):
        x2 = _layer(x2, mem2, batch, seq, n_mem, norm_g[l], w_in[l], b_in[l], a_conv_w[l],
                    a_conv_b[l], a_ln_g[l], a_ln_b[l], c_conv_w[l], c_conv_b[l], c_f_bias[l],
                    c_hn_g[l], d_conv_w[l], d_conv_b[l], d_wa[l], d_ba[l], d_wx[l], d_bx[l],
                    d_lambda[l], mem_norm_g[l], w_mkv[l], w_up[l], w_out[l], final_norm_g,
                    l == depth - 1, consts)
    return x2.reshape(batch, seq, D_MODEL)
```

```python
import functools

import jax
import jax.numpy as jnp
from jax import lax
from jax.experimental import pallas as pl
from jax.experimental.pallas import tpu as pltpu

F32 = jnp.float32
BF16 = jnp.bfloat16

D_MODEL = 1024
E_BRANCH = 512
N_BRANCH = 5
CONF_WIDTH = 31
H_B = 8
DH_B = E_BRANCH // H_B
H_C = 4
DH_C = E_BRANCH // H_C
SHORT_CONV = 4
NB_D = 8
BW_D = E_BRANCH // NB_D
LRU_C = 8.0
H_M = 4
DH_M = E_BRANCH // H_M
EPS = 1e-6

LANES = 128
SUBLANES = 8
VMEM_LIMIT = 56 * 1024 * 1024

COL_A_VAL, COL_A_GLU, COL_A_Z = 0, 1, 2
COL_B_Q, COL_B_K, COL_B_V, COL_B_Z = 3, 4, 5, 6
COL_C_Q, COL_C_K, COL_C_V, COL_C_O, COL_C_Z = 7, 8, 9, 10, 11
COL_D_X, COL_D_Z = 12, 13
COL_M_Q, COL_M_Z = 14, 15
N_MAIN = 16 * E_BRANCH
IF_WIDTH = 2 * LANES

TM_IN, TN_IN = 1024, 2048
T_CONV_A = 256
ROWS_CONV_A = 32
HALO_A = 32
BLK = 128
TQ_MEM = 512
TM_MERGE = 512
ROWS_D = 256


def _dot(a, b):
    return jnp.dot(a, b, preferred_element_type=F32)


def _dot_nt(a, b):
    return lax.dot_general(a, b, (((1,), (1,)), ((), ())), preferred_element_type=F32)


def _sigmoid(x):
    return 0.5 * jnp.tanh(0.5 * x) + 0.5


def _silu(x):
    return x * _sigmoid(x)


def _neg_softplus(x):
    return -(jnp.maximum(x, 0.0) + jnp.log(1.0 + jnp.exp(-jnp.abs(x))))


def _split_bf16(x, parts):
    out = []
    r = x
    for _ in range(parts):
        p = r.astype(BF16)
        out.append(p)
        r = r - p.astype(F32)
    return out


def _params(*sem):
    return pltpu.CompilerParams(dimension_semantics=sem, vmem_limit_bytes=VMEM_LIMIT)


def _const_spec(shape):
    nd = len(shape)
    return pl.BlockSpec(shape, lambda *_: (0,) * nd, pipeline_mode=pl.Buffered(1))


def _inproj_kernel(x_ref, g_ref, w_ref, b_ref, wif_ref, bif_ref, proj_ref, h_ref, if_ref, hs_ref):
    @pl.when(pl.program_id(1) == 0)
    def _():
        xf = x_ref[...]
        ms = jnp.mean(xf * xf, axis=-1, keepdims=True)
        hb = (xf * lax.rsqrt(ms + EPS) * g_ref[...]).astype(BF16)
        hs_ref[...] = hb
        h_ref[...] = hb
        if_ref[...] = _dot(hb, wif_ref[...]) + bif_ref[...]

    proj_ref[...] = (_dot(hs_ref[...], w_ref[...]) + b_ref[...]).astype(proj_ref.dtype)


def _inproj(x2, g, w_main, b_main, wif, bif):
    t = x2.shape[0]
    return pl.pallas_call(
        _inproj_kernel,
        grid=(t // TM_IN, N_MAIN // TN_IN),
        in_specs=[
            pl.BlockSpec((TM_IN, D_MODEL), lambda m, n: (m, 0)),
            _const_spec((1, D_MODEL)),
            pl.BlockSpec((D_MODEL, TN_IN), lambda m, n: (0, n)),
            pl.BlockSpec((1, TN_IN), lambda m, n: (0, n)),
            _const_spec((D_MODEL, IF_WIDTH)),
            _const_spec((1, IF_WIDTH)),
        ],
        out_specs=[
            pl.BlockSpec((TM_IN, TN_IN), lambda m, n: (m, n)),
            pl.BlockSpec((TM_IN, D_MODEL), lambda m, n: (m, 0)),
            pl.BlockSpec((TM_IN, IF_WIDTH), lambda m, n: (m, 0)),
        ],
        out_shape=[
            jax.ShapeDtypeStruct((t, N_MAIN), BF16),
            jax.ShapeDtypeStruct((t, D_MODEL), BF16),
            jax.ShapeDtypeStruct((t, IF_WIDTH), F32),
        ],
        scratch_shapes=[pltpu.VMEM((TM_IN, D_MODEL), BF16)],
        compiler_params=_params("parallel", "arbitrary"),
        name="inproj",
    )(x2, g, w_main, b_main, wif, bif)


def _conv_a_kernel(val_ref, glu_ref, z_ref, cw_ref, cb_ref, lg_ref, lb_ref, o_ref, upad):
    s = pl.program_id(1)

    @pl.when(s == 0)
    def _():
        upad[0:HALO_A, :] = jnp.zeros((HALO_A, E_BRANCH), F32)

    @pl.when(s > 0)
    def _():
        upad[0:HALO_A, :] = upad[T_CONV_A:T_CONV_A + HALO_A, :]

    upad[HALO_A:HALO_A + T_CONV_A, :] = val_ref[...].astype(F32) * _sigmoid(glu_ref[...].astype(F32))

    first = HALO_A - (CONF_WIDTH - 1)
    for c in range(T_CONV_A // ROWS_CONV_A):
        r0 = c * ROWS_CONV_A
        acc = jnp.broadcast_to(cb_ref[...], (ROWS_CONV_A, E_BRANCH))
        for j in range(CONF_WIDTH):
            acc = acc + cw_ref[j:j + 1, :] * upad[r0 + first + j:r0 + first + j + ROWS_CONV_A, :]
        mu = jnp.mean(acc, axis=-1, keepdims=True)
        d = acc - mu
        var = jnp.mean(d * d, axis=-1, keepdims=True)
        y = d * lax.rsqrt(var + EPS) * lg_ref[...] + lb_ref[...]
        zz = z_ref[r0:r0 + ROWS_CONV_A, :].astype(F32)
        o_ref[r0:r0 + ROWS_CONV_A, :] = (_silu(y) * _silu(zz)).astype(o_ref.dtype)


def _conv_a(proj, batch, seq, cw, cb, lg, lb):
    ns = seq // T_CONV_A

    def col(c):
        return pl.BlockSpec((T_CONV_A, E_BRANCH), lambda b, s: (b * ns + s, c))

    return pl.pallas_call(
        _conv_a_kernel,
        grid=(batch, ns),
        in_specs=[col(COL_A_VAL), col(COL_A_GLU), col(COL_A_Z),
                  _const_spec((CONF_WIDTH, E_BRANCH)), _const_spec((1, E_BRANCH)),
                  _const_spec((1, E_BRANCH)), _const_spec((1, E_BRANCH))],
        out_specs=pl.BlockSpec((T_CONV_A, E_BRANCH), lambda b, s: (b * ns + s, 0)),
        out_shape=jax.ShapeDtypeStruct((batch * seq, E_BRANCH), BF16),
        scratch_shapes=[pltpu.VMEM((HALO_A + T_CONV_A, E_BRANCH), F32)],
        compiler_params=_params("parallel", "arbitrary"),
        name="conv_a",
    )(proj, proj, proj, cw, cb, lg, lb)


def _sb_kernel(q_ref, k_ref, v_ref, zg_ref, u2_ref, o_ref,
               k2_s, v2_s, q_s, z_s, hl_s, w_s, acc_s, r_s):
    qi = pl.program_id(1)
    n_pair = H_B // 2
    n_kb = k_ref.shape[0] // BLK
    lane = lax.broadcasted_iota(jnp.int32, (BLK, LANES), 1)
    row = lax.broadcasted_iota(jnp.int32, (BLK, LANES), 0)
    strict = lane < row
    strict2 = jnp.concatenate([strict, strict], axis=1)

    @pl.when(qi == 0)
    def _():
        lane_e = lax.broadcasted_iota(jnp.int32, (BLK, E_BRANCH), 1)
        first_head = (lane_e % LANES) < DH_B

        def fill(kb, carry):
            src = pl.multiple_of(kb * BLK, BLK)
            dst = pl.multiple_of(kb * 2 * BLK, 2 * BLK)
            for src_ref, dst_ref in ((k_ref, k2_s), (v_ref, v2_s)):
                blk = src_ref[pl.ds(src, BLK), :].astype(F32)
                dst_ref[pl.ds(dst, BLK), :] = jnp.where(first_head, blk, 0.0).astype(BF16)
                dst_ref[pl.ds(dst + BLK, BLK), :] = jnp.where(first_head, 0.0, blk).astype(BF16)
            return carry

        lax.fori_loop(0, n_kb, fill, 0)

    q_s[...] = (q_ref[...].astype(F32) * (DH_B ** -0.5)).astype(BF16)

    def block_step(kb, diag):
        rows2 = pl.ds(pl.multiple_of(kb * 2 * BLK, 2 * BLK), 2 * BLK)
        for p in range(n_pair):
            lanes = slice(p * LANES, (p + 1) * LANES)
            z2 = _dot_nt(q_s[:, lanes], k2_s[rows2, lanes])
            lk = _neg_softplus(z2)
            if diag:
                lk = jnp.where(strict2, lk, 0.0)
            z_s[p] = z2
            hi, lo = _split_bf16(lk, 2)
            for hh in range(2):
                half = slice(hh * LANES, (hh + 1) * LANES)
                hl_s[2 * p + hh] = jnp.concatenate([hi[:, half], lo[:, half]], axis=1)
        for h in range(H_B):
            p, hh = divmod(h, 2)
            half = slice(hh * LANES, (hh + 1) * LANES)
            cs = _dot(hl_s[h], u2_ref[...])
            cum = cs[:, :LANES]
            tot = cs[:, LANES:]
            if diag:
                w = jnp.where(strict, jnp.exp(z_s[p, :, half] + cum), 0.0)
                r_s[h] = tot
            else:
                w = jnp.exp(z_s[p, :, half] + cum + r_s[h])
                r_s[h] += tot
            w_s[p, :, half] = w.astype(BF16)
        for p in range(n_pair):
            lanes = slice(p * LANES, (p + 1) * LANES)
            pv = _dot(w_s[p], v2_s[rows2, lanes])
            if diag:
                acc_s[p] = pv
            else:
                acc_s[p] += pv

    block_step(qi, True)

    def body(i, carry):
        block_step(qi - 1 - i, False)
        return carry

    lax.fori_loop(0, qi, body, 0)

    for p in range(n_pair):
        lanes = slice(p * LANES, (p + 1) * LANES)
        zz = zg_ref[:, lanes].astype(F32)
        o_ref[:, lanes] = (acc_s[p] * _silu(zz)).astype(o_ref.dtype)


def _sb_attn(proj, batch, seq, umat):
    nq = seq // BLK
    return pl.pallas_call(
        _sb_kernel,
        grid=(batch, nq),
        in_specs=[
            pl.BlockSpec((BLK, E_BRANCH), lambda b, q: (b * nq + q, COL_B_Q)),
            pl.BlockSpec((seq, E_BRANCH), lambda b, q: (b, COL_B_K)),
            pl.BlockSpec((seq, E_BRANCH), lambda b, q: (b, COL_B_V)),
            pl.BlockSpec((BLK, E_BRANCH), lambda b, q: (b * nq + q, COL_B_Z)),
            _const_spec((2 * BLK, 2 * LANES)),
        ],
        out_specs=pl.BlockSpec((BLK, E_BRANCH), lambda b, q: (b * nq + q, 0)),
        out_shape=jax.ShapeDtypeStruct((batch * seq, E_BRANCH), BF16),
        scratch_shapes=[pltpu.VMEM((2 * seq, E_BRANCH), BF16),
                        pltpu.VMEM((2 * seq, E_BRANCH), BF16),
                        pltpu.VMEM((BLK, E_BRANCH), BF16),
                        pltpu.VMEM((H_B // 2, BLK, 2 * LANES), F32),
                        pltpu.VMEM((H_B, BLK, 2 * LANES), BF16),
                        pltpu.VMEM((H_B // 2, BLK, 2 * LANES), BF16),
                        pltpu.VMEM((H_B // 2, BLK, LANES), F32),
                        pltpu.VMEM((H_B, BLK, LANES), F32)],
        compiler_params=_params("parallel", "arbitrary"),
        name="sb_attn",
    )(proj, proj, proj, proj, umat)


def _mlstm_kernel(q_ref, k_ref, v_ref, og_ref, z_ref, ifc_ref, ir_ref, fr_ref,
                  wq_ref, wk_ref, bq_ref, bk_ref, fbc_ref, fbr_ref, g_ref, lt_ref, ut_ref,
                  y_ref, qpad, kpad, ct_ref, m_ref):
    c = pl.program_id(1)
    halo = SUBLANES

    @pl.when(c == 0)
    def _():
        qpad[0:halo, :] = jnp.zeros((halo, E_BRANCH), F32)
        kpad[0:halo, :] = jnp.zeros((halo, E_BRANCH), F32)
        ct_ref[...] = jnp.zeros(ct_ref.shape, F32)
        m_ref[...] = jnp.zeros(m_ref.shape, F32)

    @pl.when(c > 0)
    def _():
        qpad[0:halo, :] = qpad[BLK:BLK + halo, :]
        kpad[0:halo, :] = kpad[BLK:BLK + halo, :]

    qpad[halo:halo + BLK, :] = q_ref[...].astype(F32)
    kpad[halo:halo + BLK, :] = k_ref[...].astype(F32)

    first = halo - (SHORT_CONV - 1)
    qc = jnp.broadcast_to(bq_ref[...], (BLK, E_BRANCH))
    kc = jnp.broadcast_to(bk_ref[...], (BLK, E_BRANCH))
    for j in range(SHORT_CONV):
        qc = qc + wq_ref[j:j + 1, :] * qpad[first + j:first + j + BLK, :]
        kc = kc + wk_ref[j:j + 1, :] * kpad[first + j:first + j + BLK, :]
    qc = _silu(qc).astype(BF16)
    kc = _silu(kc) * (DH_C ** -0.5)

    ic = ifc_ref[:, 0:LANES]
    fc = _neg_softplus(-(ifc_ref[:, LANES:2 * LANES] + fbc_ref[...]))
    bc = sum(_dot(lt_ref[...], part) for part in _split_bf16(fc, 3))
    ir = ir_ref[0]
    fr = _neg_softplus(-(fr_ref[0] + fbr_ref[...]))
    br = sum(_dot(part, ut_ref[...]) for part in _split_bf16(fr, 3))

    m_prev = m_ref[...]
    b_last = bc[BLK - 1:BLK, :]
    wlog = b_last - bc + ic
    m_new = jnp.maximum(b_last + m_prev, jnp.max(wlog, axis=0, keepdims=True))
    wst = jnp.exp(wlog - m_new)
    decay = jnp.exp(b_last + m_prev - m_new)
    inter = bc + m_prev

    lane = lax.broadcasted_iota(jnp.int32, (BLK, LANES), 1)
    row = lax.broadcasted_iota(jnp.int32, (BLK, LANES), 0)
    causal = lane <= row

    for h in range(H_C):
        sl = slice(h * LANES, (h + 1) * LANES)
        qh = qc[:, sl]
        kh = kc[:, sl]
        vh = v_ref[:, sl]
        s_qk = _dot_nt(qh, kh.astype(BF16))
        dlog = jnp.where(causal, bc[:, h:h + 1] - br[h:h + 1, :] + ir[h:h + 1, :], -jnp.inf)
        inter_h = inter[:, h:h + 1]
        mt = jnp.maximum(inter_h, jnp.max(dlog, axis=1, keepdims=True))
        w_intra = jnp.exp(dlog - mt)
        w_inter = jnp.exp(inter_h - mt)
        sc = s_qk * w_intra
        qct = _dot(qh, ct_ref[h].astype(BF16))
        num = _dot(sc.astype(BF16), vh) + w_inter * qct[:, :LANES]
        den = jnp.sum(sc, axis=1, keepdims=True) + w_inter * qct[:, LANES:LANES + 1]
        hh = num / jnp.maximum(jnp.abs(den), jnp.exp(-mt))
        mu = jnp.mean(hh, axis=-1, keepdims=True)
        dd = hh - mu
        var = jnp.mean(dd * dd, axis=-1, keepdims=True)
        hn = dd * lax.rsqrt(var + EPS) * g_ref[:, sl]
        y = hn * _sigmoid(og_ref[:, sl].astype(F32)) * _silu(z_ref[:, sl].astype(F32))
        y_ref[:, sl] = y.astype(y_ref.dtype)

        w_col = wst[:, h:h + 1]
        vw = jnp.concatenate([vh.astype(F32) * w_col, jnp.where(lane == 0, w_col, 0.0)], axis=1)
        k_t = kh.T.astype(BF16)
        ct_ref[h] = decay[:, h:h + 1] * ct_ref[h] + _dot(k_t, vw.astype(BF16))

    m_ref[...] = m_new


def _mlstm(proj, ifv, irow, frow, batch, seq, wq, wk, bq, bk, fbc, fbr, hn_g, ltri, utri):
    nc = seq // BLK

    def col(c):
        return pl.BlockSpec((BLK, E_BRANCH), lambda b, s: (b * nc + s, c))

    row_spec = pl.BlockSpec((1, SUBLANES, BLK), lambda b, s: (b, 0, s))
    return pl.pallas_call(
        _mlstm_kernel,
        grid=(batch, nc),
        in_specs=[col(COL_C_Q), col(COL_C_K), col(COL_C_V), col(COL_C_O), col(COL_C_Z),
                  pl.BlockSpec((BLK, IF_WIDTH), lambda b, s: (b * nc + s, 0)),
                  row_spec, row_spec,
                  _const_spec((SHORT_CONV, E_BRANCH)), _const_spec((SHORT_CONV, E_BRANCH)),
                  _const_spec((1, E_BRANCH)), _const_spec((1, E_BRANCH)),
                  _const_spec((1, LANES)), _const_spec((SUBLANES, LANES)),
                  _const_spec((1, E_BRANCH)),
                  _const_spec((BLK, BLK)), _const_spec((BLK, BLK))],
        out_specs=pl.BlockSpec((BLK, E_BRANCH), lambda b, s: (b * nc + s, 0)),
        out_shape=jax.ShapeDtypeStruct((batch * seq, E_BRANCH), BF16),
        scratch_shapes=[pltpu.VMEM((SUBLANES + BLK, E_BRANCH), F32),
                        pltpu.VMEM((SUBLANES + BLK, E_BRANCH), F32),
                        pltpu.VMEM((H_C, DH_C, 2 * LANES), F32),
                        pltpu.VMEM((1, LANES), F32)],
        compiler_params=_params("parallel", "arbitrary"),
        name="mlstm",
    )(proj, proj, proj, proj, proj, ifv, irow, frow, wq, wk, bq, bk, fbc, fbr, hn_g, ltri, utri)


def _rglru_kernel(x_ref, z_ref, cw_ref, cb_ref, wa_ref, ba_ref, wx_ref, bx_ref, lam_ref,
                  y_ref, xpad, a_s, h_s):
    seq = x_ref.shape[0]
    halo = SUBLANES
    n_grp = E_BRANCH // LANES
    xpad[0:halo, :] = jnp.zeros((halo, E_BRANCH), F32)
    xpad[halo:halo + seq, :] = x_ref[...].astype(F32)
    log_lam = _neg_softplus(-lam_ref[...])

    first = halo - (SHORT_CONV - 1)
    for c in range(seq // ROWS_D):
        r0 = c * ROWS_D
        xc = jnp.broadcast_to(cb_ref[...], (ROWS_D, E_BRANCH))
        for j in range(SHORT_CONV):
            xc = xc + cw_ref[j:j + 1, :] * xpad[r0 + first + j:r0 + first + j + ROWS_D, :]
        xb = xc.astype(BF16)
        r = _sigmoid(_dot(xb, wa_ref[...]) + ba_ref[...])
        i = _sigmoid(_dot(xb, wx_ref[...]) + bx_ref[...])
        log_a = LRU_C * r * log_lam
        a = jnp.exp(log_a)
        u = jnp.sqrt(-jnp.tanh(log_a) * (a * a + 1.0)) * (i * xc)
        for g in range(n_grp):
            a_s[g, r0:r0 + ROWS_D, :] = a[:, g * LANES:(g + 1) * LANES]
            h_s[g, r0:r0 + ROWS_D, :] = u[:, g * LANES:(g + 1) * LANES]

    seg = seq // SUBLANES

    def scan_body(t, carry):
        idx = pl.ds(t, SUBLANES, stride=seg)
        out = []
        for g in range(n_grp):
            h, ap = carry[g]
            a8 = a_s[g, idx, :]
            h = a8 * h + h_s[g, idx, :]
            ap = ap * a8
            h_s[g, idx, :] = h
            a_s[g, idx, :] = ap
            out.append((h, ap))
        return tuple(out)

    init = tuple((jnp.zeros((SUBLANES, LANES), F32), jnp.ones((SUBLANES, LANES), F32))
                 for _ in range(n_grp))
    ends = lax.fori_loop(0, seg, scan_body, init)
    carries = []
    for h_end, a_end in ends:
        carry = jnp.zeros((1, LANES), F32)
        rows = []
        for j in range(SUBLANES):
            rows.append(carry)
            carry = h_end[j:j + 1, :] + a_end[j:j + 1, :] * carry
        carries.append(jnp.concatenate(rows, axis=0))

    def fix_body(t, _):
        idx = pl.ds(t, SUBLANES, stride=seg)
        for g in range(n_grp):
            h_s[g, idx, :] = h_s[g, idx, :] + a_s[g, idx, :] * carries[g]
        return 0

    lax.fori_loop(0, seg, fix_body, 0)

    for c in range(seq // ROWS_D):
        r0 = c * ROWS_D
        hs = jnp.concatenate([h_s[g, r0:r0 + ROWS_D, :] for g in range(n_grp)], axis=1)
        zz = z_ref[r0:r0 + ROWS_D, :].astype(F32)
        y_ref[r0:r0 + ROWS_D, :] = (hs * _silu(zz)).astype(y_ref.dtype)


def _rglru(proj, batch, seq, cw, cb, wa, ba, wx, bx, lam):
    return pl.pallas_call(
        _rglru_kernel,
        grid=(batch,),
        in_specs=[pl.BlockSpec((seq, E_BRANCH), lambda b: (b, COL_D_X)),
                  pl.BlockSpec((seq, E_BRANCH), lambda b: (b, COL_D_Z)),
                  _const_spec((SHORT_CONV, E_BRANCH)), _const_spec((1, E_BRANCH)),
                  _const_spec((E_BRANCH, E_BRANCH)), _const_spec((1, E_BRANCH)),
                  _const_spec((E_BRANCH, E_BRANCH)), _const_spec((1, E_BRANCH)),
                  _const_spec((1, E_BRANCH))],
        out_specs=pl.BlockSpec((seq, E_BRANCH), lambda b: (b, 0)),
        out_shape=jax.ShapeDtypeStruct((batch * seq, E_BRANCH), BF16),
        scratch_shapes=[pltpu.VMEM((SUBLANES + seq, E_BRANCH), F32),
                        pltpu.VMEM((E_BRANCH // LANES, seq, LANES), F32),
                        pltpu.VMEM((E_BRANCH // LANES, seq, LANES), F32)],
        compiler_params=_params("parallel"),
        name="rglru",
    )(proj, proj, cw, cb, wa, ba, wx, bx, lam)


def _memkv_kernel(mem_ref, g_ref, w_ref, kv_ref):
    xf = mem_ref[...]
    ms = jnp.mean(xf * xf, axis=-1, keepdims=True)
    mn = (xf * lax.rsqrt(ms + EPS) * g_ref[...]).astype(BF16)
    kv_ref[...] = _dot(mn, w_ref[...]).astype(kv_ref.dtype)


def _memkv(mem2, batch, n_mem, g, w):
    return pl.pallas_call(
        _memkv_kernel,
        grid=(batch,),
        in_specs=[pl.BlockSpec((n_mem, D_MODEL), lambda b: (b, 0)),
                  _const_spec((1, D_MODEL)),
                  _const_spec((D_MODEL, 2 * E_BRANCH))],
        out_specs=pl.BlockSpec((n_mem, 2 * E_BRANCH), lambda b: (b, 0)),
        out_shape=jax.ShapeDtypeStruct((batch * n_mem, 2 * E_BRANCH), BF16),
        compiler_params=_params("parallel"),
        name="memkv",
    )(mem2, g, w)


def _memattn_kernel(q_ref, z_ref, kv_ref, y_ref):
    scale = DH_M ** -0.5
    for h in range(H_M):
        sl = slice(h * LANES, (h + 1) * LANES)
        kh = kv_ref[:, sl]
        vh = kv_ref[:, E_BRANCH + h * LANES:E_BRANCH + (h + 1) * LANES]
        s = _dot_nt(q_ref[:, sl], kh) * scale
        p = jnp.exp(s - jnp.max(s, axis=-1, keepdims=True))
        o = _dot(p.astype(BF16), vh) / jnp.sum(p, axis=-1, keepdims=True)
        y_ref[:, sl] = (o * _silu(z_ref[:, sl].astype(F32))).astype(y_ref.dtype)


def _memattn(proj, kv, batch, seq, n_mem):
    nq = seq // TQ_MEM
    return pl.pallas_call(
        _memattn_kernel,
        grid=(batch, nq),
        in_specs=[pl.BlockSpec((TQ_MEM, E_BRANCH), lambda b, q: (b * nq + q, COL_M_Q)),
                  pl.BlockSpec((TQ_MEM, E_BRANCH), lambda b, q: (b * nq + q, COL_M_Z)),
                  pl.BlockSpec((n_mem, 2 * E_BRANCH), lambda b, q: (b, 0))],
        out_specs=pl.BlockSpec((TQ_MEM, E_BRANCH), lambda b, q: (b * nq + q, 0)),
        out_shape=jax.ShapeDtypeStruct((batch * seq, E_BRANCH), BF16),
        compiler_params=_params("parallel", "parallel"),
        name="memattn",
    )(proj, proj, kv)


def _merge_kernel(x_ref, h_ref, ya_ref, yb_ref, yc_ref, yd_ref, ym_ref, wg_ref, bg_ref,
                  wup_ref, wout_ref, fg_ref, o_ref, *, final_norm):
    hb = h_ref[...]
    merged = None
    for n, y_ref in enumerate((ya_ref, yb_ref, yc_ref, yd_ref, ym_ref)):
        cols = slice(n * D_MODEL, (n + 1) * D_MODEL)
        gate = _sigmoid(_dot(hb, wg_ref[:, cols]) + bg_ref[:, cols])
        term = gate * _dot(y_ref[...], wup_ref[n])
        merged = term if merged is None else merged + term
    out = x_ref[...] + _dot(merged.astype(BF16), wout_ref[...])
    if final_norm:
        ms = jnp.mean(out * out, axis=-1, keepdims=True)
        out = out * lax.rsqrt(ms + EPS) * fg_ref[...]
    o_ref[...] = out


def _merge(x2, h, ys, wg, bg, wup, wout, fg, final_norm):
    t = x2.shape[0]
    row = lambda width: pl.BlockSpec((TM_MERGE, width), lambda m: (m, 0))
    return pl.pallas_call(
        functools.partial(_merge_kernel, final_norm=final_norm),
        grid=(t // TM_MERGE,),
        in_specs=[row(D_MODEL), row(D_MODEL)] + [row(E_BRANCH)] * N_BRANCH + [
            _const_spec((D_MODEL, N_BRANCH * D_MODEL)), _const_spec((1, N_BRANCH * D_MODEL)),
            _const_spec((N_BRANCH, E_BRANCH, D_MODEL)), _const_spec((D_MODEL, D_MODEL)),
            _const_spec((1, D_MODEL))],
        out_specs=row(D_MODEL),
        out_shape=jax.ShapeDtypeStruct((t, D_MODEL), F32),
        compiler_params=_params("parallel"),
        name="merge_final" if final_norm else "merge",
    )(x2, h, *ys, wg, bg, wup, wout, fg)


def _block_diag(w):
    nb, bw, _ = w.shape
    eye = jnp.eye(nb, dtype=w.dtype)
    return (eye[:, None, :, None] * w[:, :, None, :]).reshape(nb * bw, nb * bw)


def _layer(x2, mem2, batch, seq, n_mem, norm_g, w_in, b_in, a_conv_w, a_conv_b, a_ln_g, a_ln_b,
           c_conv_w, c_conv_b, c_f_bias, c_hn_g, d_conv_w, d_conv_b, d_wa, d_ba, d_wx, d_bx,
           d_lambda, mem_norm_g, w_mkv, w_up, w_out, final_g, final_norm, consts):
    umat, ltri, utri = consts
    row = lambda v: v.reshape(1, -1)
    n_pre = 10 * E_BRANCH
    n_if = 2 * H_C
    n_gate0 = N_MAIN + n_if
    w_main = jnp.concatenate([w_in[:, :n_pre], w_in[:, n_pre + n_if:n_gate0]], axis=1).astype(BF16)
    b_main = row(jnp.concatenate([b_in[:n_pre], b_in[n_pre + n_if:n_gate0]]))
    wif = jnp.zeros((D_MODEL, IF_WIDTH), F32)
    wif = wif.at[:, 0:H_C].set(w_in[:, n_pre:n_pre + H_C])
    wif = wif.at[:, LANES:LANES + H_C].set(w_in[:, n_pre + H_C:n_pre + n_if]).astype(BF16)
    bif = jnp.zeros((1, IF_WIDTH), F32)
    bif = bif.at[0, 0:H_C].set(b_in[n_pre:n_pre + H_C])
    bif = bif.at[0, LANES:LANES + H_C].set(b_in[n_pre + H_C:n_pre + n_if])

    proj, h, ifv = _inproj(x2, row(norm_g), w_main, b_main, wif, bif)

    y_a = _conv_a(proj, batch, seq, a_conv_w, row(a_conv_b), row(a_ln_g), row(a_ln_b))
    y_b = _sb_attn(proj, batch, seq, umat)

    pad_rows = ((0, 0), (0, SUBLANES - H_C), (0, 0))
    irow = jnp.pad(ifv[:, 0:H_C].reshape(batch, seq, H_C).transpose(0, 2, 1), pad_rows)
    frow = jnp.pad(ifv[:, LANES:LANES + H_C].reshape(batch, seq, H_C).transpose(0, 2, 1), pad_rows)
    fbc = jnp.zeros((1, LANES), F32).at[0, 0:H_C].set(c_f_bias)
    fbr = jnp.zeros((SUBLANES, LANES), F32).at[0:H_C, :].set(
        jnp.broadcast_to(c_f_bias[:, None], (H_C, LANES)))
    y_c = _mlstm(proj, ifv, irow, frow, batch, seq,
                 c_conv_w[:, :E_BRANCH], c_conv_w[:, E_BRANCH:],
                 row(c_conv_b[:E_BRANCH]), row(c_conv_b[E_BRANCH:]),
                 fbc, fbr, row(c_hn_g), ltri, utri)

    y_d = _rglru(proj, batch, seq, d_conv_w, row(d_conv_b),
                 _block_diag(d_wa).astype(BF16), row(d_ba),
                 _block_diag(d_wx).astype(BF16), row(d_bx), row(d_lambda))

    kv = _memkv(mem2, batch, n_mem, row(mem_norm_g), w_mkv.astype(BF16))
    y_m = _memattn(proj, kv, batch, seq, n_mem)

    return _merge(x2, h, (y_a, y_b, y_c, y_d, y_m), w_in[:, n_gate0:].astype(BF16),
                  row(b_in[n_gate0:]), w_up.astype(BF16), w_out.astype(BF16), row(final_g),
                  final_norm)


def kernel(x, mem, norm_g, w_in, b_in, a_conv_w, a_conv_b, a_ln_g, a_ln_b, c_conv_w, c_conv_b,
           c_f_bias, c_hn_g, d_conv_w, d_conv_b, d_wa, d_ba, d_wx, d_bx, d_lambda, mem_norm_g,
           w_mkv, w_up, w_out, final_norm_g):
    batch, seq, _ = x.shape
    n_mem = mem.shape[1]
    depth = w_in.shape[0]
    assert seq % T_CONV_A == 0 and seq % TQ_MEM == 0 and seq % ROWS_D == 0
    assert (batch * seq) % TM_IN == 0 and (batch * seq) % TM_MERGE == 0

    tri = jnp.tril(jnp.ones((BLK, BLK), F32))
    umat = jnp.concatenate([tri, jnp.ones((BLK, LANES), F32)], axis=1)
    umat = jnp.concatenate([umat, umat], axis=0).astype(BF16)
    consts = (umat, tri.astype(BF16), tri.T.astype(BF16))

    x2 = x.reshape(batch * seq, D_MODEL)
    mem2 = mem.reshape(batch * n_mem, D_MODEL)
    for l in range(depth):
        x2 = _layer(x2, mem2, batch, seq, n_mem, norm_g[l], w_in[l], b_in[l], a_conv_w[l],
                    a_conv_b[l], a_ln_g[l], a_ln_b[l], c_conv_w[l], c_conv_b[l], c_f_bias[l],
                    c_hn_g[l], d_conv_w[l], d_conv_b[l], d_wa[l], d_ba[l], d_wx[l], d_bx[l],
                    d_lambda[l], mem_norm_g[l], w_mkv[l], w_up[l], w_out[l], final_norm_g,
                    l == depth - 1, consts)
    return x2.reshape(batch, seq, D_MODEL)
```

```python
import functools

import jax
import jax.numpy as jnp
from jax import lax
from jax.experimental import pallas as pl
from jax.experimental.pallas import tpu as pltpu

F32 = jnp.float32
BF16 = jnp.bfloat16

D_MODEL = 1024
E_BRANCH = 512
N_BRANCH = 5
CONF_WIDTH = 31
H_B = 8
DH_B = E_BRANCH // H_B
H_C = 4
DH_C = E_BRANCH // H_C
SHORT_CONV = 4
NB_D = 8
BW_D = E_BRANCH // NB_D
LRU_C = 8.0
H_M = 4
DH_M = E_BRANCH // H_M
EPS = 1e-6

LANES = 128
SUBLANES = 8
VMEM_LIMIT = 56 * 1024 * 1024

COL_A_VAL, COL_A_GLU, COL_A_Z = 0, 1, 2
COL_B_Q, COL_B_K, COL_B_V, COL_B_Z = 3, 4, 5, 6
COL_C_Q, COL_C_K, COL_C_V, COL_C_O, COL_C_Z = 7, 8, 9, 10, 11
COL_D_X, COL_D_Z = 12, 13
COL_M_Q, COL_M_Z = 14, 15
N_MAIN = 16 * E_BRANCH
IF_WIDTH = 2 * LANES

TM_IN, TN_IN = 1024, 2048
T_CONV_A = 256
ROWS_CONV_A = 32
HALO_A = 32
BLK = 128
SB_GROUP = 4
MLSTM_CHUNKS = 2
TQ_MEM = 512
TM_MERGE = 512
ROWS_D = 256


def _dot(a, b):
    return jnp.dot(a, b, preferred_element_type=F32)


def _dot_nt(a, b):
    return lax.dot_general(a, b, (((1,), (1,)), ((), ())), preferred_element_type=F32)


def _sigmoid(x):
    return 0.5 * jnp.tanh(0.5 * x) + 0.5


def _silu(x):
    return x * _sigmoid(x)


def _neg_softplus(x):
    return -(jnp.maximum(x, 0.0) + jnp.log(1.0 + jnp.exp(-jnp.abs(x))))


def _split_bf16(x, parts):
    out = []
    r = x
    for _ in range(parts):
        p = r.astype(BF16)
        out.append(p)
        r = r - p.astype(F32)
    return out


def _params(*sem):
    return pltpu.CompilerParams(dimension_semantics=sem, vmem_limit_bytes=VMEM_LIMIT)


def _const_spec(shape):
    nd = len(shape)
    return pl.BlockSpec(shape, lambda *_: (0,) * nd, pipeline_mode=pl.Buffered(1))


def _inproj_kernel(x_ref, g_ref, w_ref, b_ref, wif_ref, bif_ref, proj_ref, h_ref, if_ref, hs_ref):
    @pl.when(pl.program_id(1) == 0)
    def _():
        xf = x_ref[...]
        ms = jnp.mean(xf * xf, axis=-1, keepdims=True)
        hb = (xf * lax.rsqrt(ms + EPS) * g_ref[...]).astype(BF16)
        hs_ref[...] = hb
        h_ref[...] = hb
        if_ref[...] = _dot(hb, wif_ref[...]) + bif_ref[...]

    proj_ref[...] = (_dot(hs_ref[...], w_ref[...]) + b_ref[...]).astype(proj_ref.dtype)


def _inproj(x2, g, w_main, b_main, wif, bif):
    t = x2.shape[0]
    return pl.pallas_call(
        _inproj_kernel,
        grid=(t // TM_IN, N_MAIN // TN_IN),
        in_specs=[
            pl.BlockSpec((TM_IN, D_MODEL), lambda m, n: (m, 0)),
            _const_spec((1, D_MODEL)),
            pl.BlockSpec((D_MODEL, TN_IN), lambda m, n: (0, n)),
            pl.BlockSpec((1, TN_IN), lambda m, n: (0, n)),
            _const_spec((D_MODEL, IF_WIDTH)),
            _const_spec((1, IF_WIDTH)),
        ],
        out_specs=[
            pl.BlockSpec((TM_IN, TN_IN), lambda m, n: (m, n)),
            pl.BlockSpec((TM_IN, D_MODEL), lambda m, n: (m, 0)),
            pl.BlockSpec((TM_IN, IF_WIDTH), lambda m, n: (m, 0)),
        ],
        out_shape=[
            jax.ShapeDtypeStruct((t, N_MAIN), BF16),
            jax.ShapeDtypeStruct((t, D_MODEL), BF16),
            jax.ShapeDtypeStruct((t, IF_WIDTH), F32),
        ],
        scratch_shapes=[pltpu.VMEM((TM_IN, D_MODEL), BF16)],
        compiler_params=_params("parallel", "arbitrary"),
        name="inproj",
    )(x2, g, w_main, b_main, wif, bif)


def _conv_a_kernel(val_ref, glu_ref, z_ref, cw_ref, cb_ref, lg_ref, lb_ref, o_ref, upad, ush):
    s = pl.program_id(1)

    @pl.when(s == 0)
    def _():
        upad[0:HALO_A, :] = jnp.zeros((HALO_A, E_BRANCH), F32)

    @pl.when(s > 0)
    def _():
        upad[0:HALO_A, :] = upad[T_CONV_A:T_CONV_A + HALO_A, :]

    upad[HALO_A:HALO_A + T_CONV_A, :] = val_ref[...].astype(F32) * _sigmoid(glu_ref[...].astype(F32))

    n_rows = HALO_A + T_CONV_A
    for r in range(1, SUBLANES):
        ush[r - 1, SUBLANES:n_rows, :] = upad[SUBLANES - r:n_rows - r, :]

    for c in range(T_CONV_A // ROWS_CONV_A):
        r0 = c * ROWS_CONV_A
        acc = jnp.broadcast_to(cb_ref[...], (ROWS_CONV_A, E_BRANCH))
        for j in range(CONF_WIDTH):
            tiles_back, r = divmod(CONF_WIDTH - 1 - j, SUBLANES)
            start = HALO_A + r0 - SUBLANES * tiles_back
            if r == 0:
                src = upad[start:start + ROWS_CONV_A, :]
            else:
                src = ush[r - 1, start:start + ROWS_CONV_A, :]
            acc = acc + cw_ref[j:j + 1, :] * src
        mu = jnp.mean(acc, axis=-1, keepdims=True)
        d = acc - mu
        var = jnp.mean(d * d, axis=-1, keepdims=True)
        y = d * lax.rsqrt(var + EPS) * lg_ref[...] + lb_ref[...]
        zz = z_ref[r0:r0 + ROWS_CONV_A, :].astype(F32)
        o_ref[r0:r0 + ROWS_CONV_A, :] = (_silu(y) * _silu(zz)).astype(o_ref.dtype)


def _conv_a(proj, batch, seq, cw, cb, lg, lb):
    ns = seq // T_CONV_A

    def col(c):
        return pl.BlockSpec((T_CONV_A, E_BRANCH), lambda b, s: (b * ns + s, c))

    return pl.pallas_call(
        _conv_a_kernel,
        grid=(batch, ns),
        in_specs=[col(COL_A_VAL), col(COL_A_GLU), col(COL_A_Z),
                  _const_spec((CONF_WIDTH, E_BRANCH)), _const_spec((1, E_BRANCH)),
                  _const_spec((1, E_BRANCH)), _const_spec((1, E_BRANCH))],
        out_specs=pl.BlockSpec((T_CONV_A, E_BRANCH), lambda b, s: (b * ns + s, 0)),
        out_shape=jax.ShapeDtypeStruct((batch * seq, E_BRANCH), BF16),
        scratch_shapes=[pltpu.VMEM((HALO_A + T_CONV_A, E_BRANCH), F32),
                        pltpu.VMEM((SUBLANES - 1, HALO_A + T_CONV_A, E_BRANCH), F32)],
        compiler_params=_params("parallel", "arbitrary"),
        name="conv_a",
    )(proj, proj, proj, cw, cb, lg, lb)


def _sb_kernel(q_ref, k_ref, v_ref, zg_ref, u2_ref, o_ref,
               k2_s, v2_s, q_s, z_s, hl_s, w_s, acc_s, r_s):
    qi = pl.program_id(1)
    n_pair = H_B // 2
    n_kb = k_ref.shape[0] // BLK
    lane = lax.broadcasted_iota(jnp.int32, (BLK, LANES), 1)
    row = lax.broadcasted_iota(jnp.int32, (BLK, LANES), 0)
    strict = lane < row
    strict2 = jnp.concatenate([strict, strict], axis=1)

    @pl.when(qi == 0)
    def _():
        lane_e = lax.broadcasted_iota(jnp.int32, (BLK, E_BRANCH), 1)
        first_head = (lane_e % LANES) < DH_B

        def fill(kb, carry):
            src = pl.multiple_of(kb * BLK, BLK)
            dst = pl.multiple_of(kb * 2 * BLK, 2 * BLK)
            for src_ref, dst_ref in ((k_ref, k2_s), (v_ref, v2_s)):
                blk = src_ref[pl.ds(src, BLK), :].astype(F32)
                dst_ref[pl.ds(dst, BLK), :] = jnp.where(first_head, blk, 0.0).astype(BF16)
                dst_ref[pl.ds(dst + BLK, BLK), :] = jnp.where(first_head, 0.0, blk).astype(BF16)
            return carry

        lax.fori_loop(0, n_kb, fill, 0)

    q_s[...] = (q_ref[...].astype(F32) * (DH_B ** -0.5)).astype(BF16)

    def multi_step(kb_top, n_blk, diag):
        base = pl.multiple_of((kb_top - (n_blk - 1)) * 2 * BLK, 2 * BLK)
        for k in range(n_blk):
            rows2 = pl.ds(base + (n_blk - 1 - k) * 2 * BLK, 2 * BLK)
            for p in range(n_pair):
                lanes = slice(p * LANES, (p + 1) * LANES)
                z2 = _dot_nt(q_s[:, lanes], k2_s[rows2, lanes])
                lk = _neg_softplus(z2)
                if diag:
                    lk = jnp.where(strict2, lk, 0.0)
                z_s[k, p] = z2
                hi, lo = _split_bf16(lk, 2)
                for hh in range(2):
                    half = slice(hh * LANES, (hh + 1) * LANES)
                    hl_s[k, 2 * p + hh] = jnp.concatenate([hi[:, half], lo[:, half]], axis=1)
        for h in range(H_B):
            p, hh = divmod(h, 2)
            half = slice(hh * LANES, (hh + 1) * LANES)
            r = None if diag else r_s[h]
            for k in range(n_blk):
                cs = _dot(hl_s[k, h], u2_ref[...])
                cum = cs[:, :LANES]
                tot = cs[:, LANES:]
                if diag:
                    w = jnp.where(strict, jnp.exp(z_s[k, p, :, half] + cum), 0.0)
                    r = tot
                else:
                    w = jnp.exp(z_s[k, p, :, half] + cum + r)
                    r = r + tot
                col = (n_blk - 1 - k) * 2 * LANES + hh * LANES
                w_s[p, :, col:col + LANES] = w.astype(BF16)
            r_s[h] = r
        for p in range(n_pair):
            lanes = slice(p * LANES, (p + 1) * LANES)
            pv = _dot(w_s[p, :, :n_blk * 2 * LANES], v2_s[pl.ds(base, n_blk * 2 * BLK), lanes])
            if diag:
                acc_s[p] = pv
            else:
                acc_s[p] += pv

    multi_step(qi, 1, True)

    n_full = qi // SB_GROUP

    def body(i, carry):
        multi_step(qi - 1 - SB_GROUP * i, SB_GROUP, False)
        return carry

    lax.fori_loop(0, n_full, body, 0)

    top = qi - 1 - SB_GROUP * n_full
    rem = qi - SB_GROUP * n_full
    size = SB_GROUP // 2
    while size >= 1:
        take = (rem & size) != 0

        @pl.when(take)
        def _(top=top, size=size):
            multi_step(top, size, False)

        top = top - jnp.where(take, size, 0)
        size //= 2

    for p in range(n_pair):
        lanes = slice(p * LANES, (p + 1) * LANES)
        zz = zg_ref[:, lanes].astype(F32)
        o_ref[:, lanes] = (acc_s[p] * _silu(zz)).astype(o_ref.dtype)


def _sb_attn(proj, batch, seq, umat):
    nq = seq // BLK
    return pl.pallas_call(
        _sb_kernel,
        grid=(batch, nq),
        in_specs=[
            pl.BlockSpec((BLK, E_BRANCH), lambda b, q: (b * nq + q, COL_B_Q)),
            pl.BlockSpec((seq, E_BRANCH), lambda b, q: (b, COL_B_K)),
            pl.BlockSpec((seq, E_BRANCH), lambda b, q: (b, COL_B_V)),
            pl.BlockSpec((BLK, E_BRANCH), lambda b, q: (b * nq + q, COL_B_Z)),
            _const_spec((2 * BLK, 2 * LANES)),
        ],
        out_specs=pl.BlockSpec((BLK, E_BRANCH), lambda b, q: (b * nq + q, 0)),
        out_shape=jax.ShapeDtypeStruct((batch * seq, E_BRANCH), BF16),
        scratch_shapes=[pltpu.VMEM((2 * seq, E_BRANCH), BF16),
                        pltpu.VMEM((2 * seq, E_BRANCH), BF16),
                        pltpu.VMEM((BLK, E_BRANCH), BF16),
                        pltpu.VMEM((SB_GROUP, H_B // 2, BLK, 2 * LANES), F32),
                        pltpu.VMEM((SB_GROUP, H_B, BLK, 2 * LANES), BF16),
                        pltpu.VMEM((H_B // 2, BLK, SB_GROUP * 2 * LANES), BF16),
                        pltpu.VMEM((H_B // 2, BLK, LANES), F32),
                        pltpu.VMEM((H_B, BLK, LANES), F32)],
        compiler_params=_params("parallel", "arbitrary"),
        name="sb_attn",
    )(proj, proj, proj, proj, umat)


def _mlstm_kernel(q_ref, k_ref, v_ref, og_ref, z_ref, ifc_ref, ir_ref, fr_ref,
                  wq_ref, wk_ref, bq_ref, bk_ref, fbc_ref, fbr_ref, g_ref, lt_ref, ut_ref,
                  y_ref, qpad, kpad, ct_ref, m_ref):
    step = pl.program_id(1)
    halo = SUBLANES
    rows = MLSTM_CHUNKS * BLK

    @pl.when(step == 0)
    def _():
        qpad[0:halo, :] = jnp.zeros((halo, E_BRANCH), F32)
        kpad[0:halo, :] = jnp.zeros((halo, E_BRANCH), F32)
        ct_ref[...] = jnp.zeros(ct_ref.shape, F32)
        m_ref[...] = jnp.zeros(m_ref.shape, F32)

    @pl.when(step > 0)
    def _():
        qpad[0:halo, :] = qpad[rows:rows + halo, :]
        kpad[0:halo, :] = kpad[rows:rows + halo, :]

    qpad[halo:halo + rows, :] = q_ref[...].astype(F32)
    kpad[halo:halo + rows, :] = k_ref[...].astype(F32)

    lane = lax.broadcasted_iota(jnp.int32, (BLK, LANES), 1)
    row = lax.broadcasted_iota(jnp.int32, (BLK, LANES), 0)
    causal = lane <= row
    first = halo - (SHORT_CONV - 1)
    chunks = range(MLSTM_CHUNKS)
    heads = range(H_C)
    sls = [slice(h * LANES, (h + 1) * LANES) for h in heads]

    qc, kc, bc, br, ir, wst, decay, inter = [], [], [], [], [], [], [], []
    m_prev = m_ref[...]
    for c in chunks:
        r0 = c * BLK
        q_acc = jnp.broadcast_to(bq_ref[...], (BLK, E_BRANCH))
        k_acc = jnp.broadcast_to(bk_ref[...], (BLK, E_BRANCH))
        for j in range(SHORT_CONV):
            q_acc = q_acc + wq_ref[j:j + 1, :] * qpad[r0 + first + j:r0 + first + j + BLK, :]
            k_acc = k_acc + wk_ref[j:j + 1, :] * kpad[r0 + first + j:r0 + first + j + BLK, :]
        qc.append(_silu(q_acc).astype(BF16))
        kc.append(_silu(k_acc) * (DH_C ** -0.5))

        ic = ifc_ref[r0:r0 + BLK, 0:LANES]
        fc = _neg_softplus(-(ifc_ref[r0:r0 + BLK, LANES:2 * LANES] + fbc_ref[...]))
        bc.append(sum(_dot(lt_ref[...], part) for part in _split_bf16(fc, 3)))
        ir.append(ir_ref[0, :, r0:r0 + BLK])
        fr = _neg_softplus(-(fr_ref[0, :, r0:r0 + BLK] + fbr_ref[...]))
        br.append(sum(_dot(part, ut_ref[...]) for part in _split_bf16(fr, 3)))

        b_last = bc[c][BLK - 1:BLK, :]
        wlog = b_last - bc[c] + ic
        m_new = jnp.maximum(b_last + m_prev, jnp.max(wlog, axis=0, keepdims=True))
        wst.append(jnp.exp(wlog - m_new))
        decay.append(jnp.exp(b_last + m_prev - m_new))
        inter.append(bc[c] + m_prev)
        m_prev = m_new
    m_ref[...] = m_prev

    s_qk, qct = [], []
    for c in chunks:
        r0 = c * BLK
        s_qk.append([_dot_nt(qc[c][:, sls[h]], kc[c][:, sls[h]].astype(BF16)) for h in heads])
        qct.append([_dot(qc[c][:, sls[h]], ct_ref[h].astype(BF16)) for h in heads])
        for h in heads:
            w_col = wst[c][:, h:h + 1]
            vw = jnp.concatenate([v_ref[r0:r0 + BLK, sls[h]].astype(F32) * w_col,
                                  jnp.where(lane == 0, w_col, 0.0)], axis=1)
            k_t = kc[c][:, sls[h]].T.astype(BF16)
            ct_ref[h] = decay[c][:, h:h + 1] * ct_ref[h] + _dot(k_t, vw.astype(BF16))

    sc, mt, w_inter = [], [], []
    for c in chunks:
        sc.append([]), mt.append([]), w_inter.append([])
        for h in heads:
            dlog = jnp.where(causal, bc[c][:, h:h + 1] - br[c][h:h + 1, :] + ir[c][h:h + 1, :],
                             -jnp.inf)
            inter_h = inter[c][:, h:h + 1]
            mt[c].append(jnp.maximum(inter_h, jnp.max(dlog, axis=1, keepdims=True)))
            w_inter[c].append(jnp.exp(inter_h - mt[c][h]))
            sc[c].append(s_qk[c][h] * jnp.exp(dlog - mt[c][h]))
    num_intra = [[_dot(sc[c][h].astype(BF16), v_ref[c * BLK:(c + 1) * BLK, sls[h]]) for h in heads]
                 for c in chunks]
    units = [(c, h) for c in chunks for h in heads]
    den = {u: jnp.sum(sc[u[0]][u[1]], axis=1, keepdims=True) for u in units}
    hh = {}
    for c, h in units:
        num = num_intra[c][h] + w_inter[c][h] * qct[c][h][:, :LANES]
        dn = den[c, h] + w_inter[c][h] * qct[c][h][:, LANES:LANES + 1]
        hh[c, h] = num / jnp.maximum(jnp.abs(dn), jnp.exp(-mt[c][h]))
    mu = {u: jnp.mean(hh[u], axis=-1, keepdims=True) for u in units}
    dd = {u: hh[u] - mu[u] for u in units}
    var = {u: jnp.mean(dd[u] * dd[u], axis=-1, keepdims=True) for u in units}
    for c, h in units:
        rws = slice(c * BLK, (c + 1) * BLK)
        hn = dd[c, h] * lax.rsqrt(var[c, h] + EPS) * g_ref[:, sls[h]]
        y = (hn * _sigmoid(og_ref[rws, sls[h]].astype(F32))
             * _silu(z_ref[rws, sls[h]].astype(F32)))
        y_ref[rws, sls[h]] = y.astype(y_ref.dtype)


def _mlstm(proj, ifv, irow, frow, batch, seq, wq, wk, bq, bk, fbc, fbr, hn_g, ltri, utri):
    rows = MLSTM_CHUNKS * BLK
    nc = seq // rows

    def col(c):
        return pl.BlockSpec((rows, E_BRANCH), lambda b, s: (b * nc + s, c))

    row_spec = pl.BlockSpec((1, SUBLANES, rows), lambda b, s: (b, 0, s))
    return pl.pallas_call(
        _mlstm_kernel,
        grid=(batch, nc),
        in_specs=[col(COL_C_Q), col(COL_C_K), col(COL_C_V), col(COL_C_O), col(COL_C_Z),
                  pl.BlockSpec((rows, IF_WIDTH), lambda b, s: (b * nc + s, 0)),
                  row_spec, row_spec,
                  _const_spec((SHORT_CONV, E_BRANCH)), _const_spec((SHORT_CONV, E_BRANCH)),
                  _const_spec((1, E_BRANCH)), _const_spec((1, E_BRANCH)),
                  _const_spec((1, LANES)), _const_spec((SUBLANES, LANES)),
                  _const_spec((1, E_BRANCH)),
                  _const_spec((BLK, BLK)), _const_spec((BLK, BLK))],
        out_specs=pl.BlockSpec((rows, E_BRANCH), lambda b, s: (b * nc + s, 0)),
        out_shape=jax.ShapeDtypeStruct((batch * seq, E_BRANCH), BF16),
        scratch_shapes=[pltpu.VMEM((SUBLANES + rows, E_BRANCH), F32),
                        pltpu.VMEM((SUBLANES + rows, E_BRANCH), F32),
                        pltpu.VMEM((H_C, DH_C, 2 * LANES), F32),
                        pltpu.VMEM((1, LANES), F32)],
        compiler_params=_params("parallel", "arbitrary"),
        name="mlstm",
    )(proj, proj, proj, proj, proj, ifv, irow, frow, wq, wk, bq, bk, fbc, fbr, hn_g, ltri, utri)


def _rglru_kernel(x_ref, z_ref, cw_ref, cb_ref, wa_ref, ba_ref, wx_ref, bx_ref, lam_ref,
                  y_ref, xpad, a_s, h_s):
    seq = x_ref.shape[0]
    halo = SUBLANES
    xpad[0:halo, :] = jnp.zeros((halo, E_BRANCH), F32)
    xpad[halo:halo + seq, :] = x_ref[...].astype(F32)
    log_lam = _neg_softplus(-lam_ref[...])

    first = halo - (SHORT_CONV - 1)
    for c in range(seq // ROWS_D):
        r0 = c * ROWS_D
        xc = jnp.broadcast_to(cb_ref[...], (ROWS_D, E_BRANCH))
        for j in range(SHORT_CONV):
            xc = xc + cw_ref[j:j + 1, :] * xpad[r0 + first + j:r0 + first + j + ROWS_D, :]
        xb = xc.astype(BF16)
        r = _sigmoid(_dot(xb, wa_ref[...]) + ba_ref[...])
        i = _sigmoid(_dot(xb, wx_ref[...]) + bx_ref[...])
        log_a = LRU_C * r * log_lam
        a = jnp.exp(log_a)
        u = jnp.sqrt(-jnp.tanh(log_a) * (a * a + 1.0)) * (i * xc)
        a_s[r0:r0 + ROWS_D, :] = a
        h_s[r0:r0 + ROWS_D, :] = u

    sub = lax.broadcasted_iota(jnp.int32, (SUBLANES, E_BRANCH), 0)
    rows_out = 2 * SUBLANES

    def scan_body(g, carry):
        r0 = pl.multiple_of(g * rows_out, rows_out)
        tiles = []
        for part in range(rows_out // SUBLANES):
            rows = pl.ds(r0 + part * SUBLANES, SUBLANES)
            a = a_s[rows, :]
            u = h_s[rows, :]
            for d in (1, 2, 4):
                keep = sub >= d
                u_prev = jnp.where(keep, pltpu.roll(u, d, 0), 0.0)
                a_prev = jnp.where(keep, pltpu.roll(a, d, 0), 1.0)
                u = u + a * u_prev
                a = a * a_prev
            h = u + a * carry
            carry = jnp.broadcast_to(h[SUBLANES - 1:SUBLANES, :], (SUBLANES, E_BRANCH))
            tiles.append(h)
        hs = jnp.concatenate(tiles, axis=0)
        zz = z_ref[pl.ds(r0, rows_out), :].astype(F32)
        y_ref[pl.ds(r0, rows_out), :] = (hs * _silu(zz)).astype(y_ref.dtype)
        return carry

    lax.fori_loop(0, seq // rows_out, scan_body, jnp.zeros((SUBLANES, E_BRANCH), F32))


def _rglru(proj, batch, seq, cw, cb, wa, ba, wx, bx, lam):
    return pl.pallas_call(
        _rglru_kernel,
        grid=(batch,),
        in_specs=[pl.BlockSpec((seq, E_BRANCH), lambda b: (b, COL_D_X)),
                  pl.BlockSpec((seq, E_BRANCH), lambda b: (b, COL_D_Z)),
                  _const_spec((SHORT_CONV, E_BRANCH)), _const_spec((1, E_BRANCH)),
                  _const_spec((E_BRANCH, E_BRANCH)), _const_spec((1, E_BRANCH)),
                  _const_spec((E_BRANCH, E_BRANCH)), _const_spec((1, E_BRANCH)),
                  _const_spec((1, E_BRANCH))],
        out_specs=pl.BlockSpec((seq, E_BRANCH), lambda b: (b, 0)),
        out_shape=jax.ShapeDtypeStruct((batch * seq, E_BRANCH), BF16),
        scratch_shapes=[pltpu.VMEM((SUBLANES + seq, E_BRANCH), F32),
                        pltpu.VMEM((seq, E_BRANCH), F32),
                        pltpu.VMEM((seq, E_BRANCH), F32)],
        compiler_params=_params("parallel"),
        name="rglru",
    )(proj, proj, cw, cb, wa, ba, wx, bx, lam)


def _memkv_kernel(mem_ref, g_ref, w_ref, kv_ref):
    xf = mem_ref[...]
    ms = jnp.mean(xf * xf, axis=-1, keepdims=True)
    mn = (xf * lax.rsqrt(ms + EPS) * g_ref[...]).astype(BF16)
    kv_ref[...] = _dot(mn, w_ref[...]).astype(kv_ref.dtype)


def _memkv(mem2, batch, n_mem, g, w):
    return pl.pallas_call(
        _memkv_kernel,
        grid=(batch,),
        in_specs=[pl.BlockSpec((n_mem, D_MODEL), lambda b: (b, 0)),
                  _const_spec((1, D_MODEL)),
                  _const_spec((D_MODEL, 2 * E_BRANCH))],
        out_specs=pl.BlockSpec((n_mem, 2 * E_BRANCH), lambda b: (b, 0)),
        out_shape=jax.ShapeDtypeStruct((batch * n_mem, 2 * E_BRANCH), BF16),
        compiler_params=_params("parallel"),
        name="memkv",
    )(mem2, g, w)


def _memattn_kernel(q_ref, z_ref, kv_ref, y_ref):
    scale = DH_M ** -0.5
    for h in range(H_M):
        sl = slice(h * LANES, (h + 1) * LANES)
        kh = kv_ref[:, sl]
        vh = kv_ref[:, E_BRANCH + h * LANES:E_BRANCH + (h + 1) * LANES]
        s = _dot_nt(q_ref[:, sl], kh) * scale
        p = jnp.exp(s - jnp.max(s, axis=-1, keepdims=True))
        o = _dot(p.astype(BF16), vh) / jnp.sum(p, axis=-1, keepdims=True)
        y_ref[:, sl] = (o * _silu(z_ref[:, sl].astype(F32))).astype(y_ref.dtype)


def _memattn(proj, kv, batch, seq, n_mem):
    nq = seq // TQ_MEM
    return pl.pallas_call(
        _memattn_kernel,
        grid=(batch, nq),
        in_specs=[pl.BlockSpec((TQ_MEM, E_BRANCH), lambda b, q: (b * nq + q, COL_M_Q)),
                  pl.BlockSpec((TQ_MEM, E_BRANCH), lambda b, q: (b * nq + q, COL_M_Z)),
                  pl.BlockSpec((n_mem, 2 * E_BRANCH), lambda b, q: (b, 0))],
        out_specs=pl.BlockSpec((TQ_MEM, E_BRANCH), lambda b, q: (b * nq + q, 0)),
        out_shape=jax.ShapeDtypeStruct((batch * seq, E_BRANCH), BF16),
        compiler_params=_params("parallel", "parallel"),
        name="memattn",
    )(proj, proj, kv)


def _merge_kernel(x_ref, h_ref, ya_ref, yb_ref, yc_ref, yd_ref, ym_ref, wg_ref, bg_ref,
                  wup_ref, wout_ref, fg_ref, o_ref, *, final_norm):
    hb = h_ref[...]
    merged = None
    for n, y_ref in enumerate((ya_ref, yb_ref, yc_ref, yd_ref, ym_ref)):
        cols = slice(n * D_MODEL, (n + 1) * D_MODEL)
        gate = _sigmoid(_dot(hb, wg_ref[:, cols]) + bg_ref[:, cols])
        term = gate * _dot(y_ref[...], wup_ref[n])
        merged = term if merged is None else merged + term
    out = x_ref[...] + _dot(merged.astype(BF16), wout_ref[...])
    if final_norm:
        ms = jnp.mean(out * out, axis=-1, keepdims=True)
        out = out * lax.rsqrt(ms + EPS) * fg_ref[...]
    o_ref[...] = out


def _merge(x2, h, ys, wg, bg, wup, wout, fg, final_norm):
    t = x2.shape[0]
    row = lambda width: pl.BlockSpec((TM_MERGE, width), lambda m: (m, 0))
    return pl.pallas_call(
        functools.partial(_merge_kernel, final_norm=final_norm),
        grid=(t // TM_MERGE,),
        in_specs=[row(D_MODEL), row(D_MODEL)] + [row(E_BRANCH)] * N_BRANCH + [
            _const_spec((D_MODEL, N_BRANCH * D_MODEL)), _const_spec((1, N_BRANCH * D_MODEL)),
            _const_spec((N_BRANCH, E_BRANCH, D_MODEL)), _const_spec((D_MODEL, D_MODEL)),
            _const_spec((1, D_MODEL))],
        out_specs=row(D_MODEL),
        out_shape=jax.ShapeDtypeStruct((t, D_MODEL), F32),
        compiler_params=_params("parallel"),
        name="merge_final" if final_norm else "merge",
    )(x2, h, *ys, wg, bg, wup, wout, fg)


def _block_diag(w):
    nb, bw, _ = w.shape
    eye = jnp.eye(nb, dtype=w.dtype)
    return (eye[:, None, :, None] * w[:, :, None, :]).reshape(nb * bw, nb * bw)


def _layer(x2, mem2, batch, seq, n_mem, norm_g, w_in, b_in, a_conv_w, a_conv_b, a_ln_g, a_ln_b,
           c_conv_w, c_conv_b, c_f_bias, c_hn_g, d_conv_w, d_conv_b, d_wa, d_ba, d_wx, d_bx,
           d_lambda, mem_norm_g, w_mkv, w_up, w_out, final_g, final_norm, consts):
    umat, ltri, utri = consts
    row = lambda v: v.reshape(1, -1)
    n_pre = 10 * E_BRANCH
    n_if = 2 * H_C
    n_gate0 = N_MAIN + n_if
    w_main = jnp.concatenate([w_in[:, :n_pre], w_in[:, n_pre + n_if:n_gate0]], axis=1).astype(BF16)
    b_main = row(jnp.concatenate([b_in[:n_pre], b_in[n_pre + n_if:n_gate0]]))
    wif = jnp.zeros((D_MODEL, IF_WIDTH), F32)
    wif = wif.at[:, 0:H_C].set(w_in[:, n_pre:n_pre + H_C])
    wif = wif.at[:, LANES:LANES + H_C].set(w_in[:, n_pre + H_C:n_pre + n_if]).astype(BF16)
    bif = jnp.zeros((1, IF_WIDTH), F32)
    bif = bif.at[0, 0:H_C].set(b_in[n_pre:n_pre + H_C])
    bif = bif.at[0, LANES:LANES + H_C].set(b_in[n_pre + H_C:n_pre + n_if])

    proj, h, ifv = _inproj(x2, row(norm_g), w_main, b_main, wif, bif)

    y_a = _conv_a(proj, batch, seq, a_conv_w, row(a_conv_b), row(a_ln_g), row(a_ln_b))
    y_b = _sb_attn(proj, batch, seq, umat)

    pad_rows = ((0, 0), (0, SUBLANES - H_C), (0, 0))
    irow = jnp.pad(ifv[:, 0:H_C].reshape(batch, seq, H_C).transpose(0, 2, 1), pad_rows)
    frow = jnp.pad(ifv[:, LANES:LANES + H_C].reshape(batch, seq, H_C).transpose(0, 2, 1), pad_rows)
    fbc = jnp.zeros((1, LANES), F32).at[0, 0:H_C].set(c_f_bias)
    fbr = jnp.zeros((SUBLANES, LANES), F32).at[0:H_C, :].set(
        jnp.broadcast_to(c_f_bias[:, None], (H_C, LANES)))
    y_c = _mlstm(proj, ifv, irow, frow, batch, seq,
                 c_conv_w[:, :E_BRANCH], c_conv_w[:, E_BRANCH:],
                 row(c_conv_b[:E_BRANCH]), row(c_conv_b[E_BRANCH:]),
                 fbc, fbr, row(c_hn_g), ltri, utri)

    y_d = _rglru(proj, batch, seq, d_conv_w, row(d_conv_b),
                 _block_diag(d_wa).astype(BF16), row(d_ba),
                 _block_diag(d_wx).astype(BF16), row(d_bx), row(d_lambda))

    kv = _memkv(mem2, batch, n_mem, row(mem_norm_g), w_mkv.astype(BF16))
    y_m = _memattn(proj, kv, batch, seq, n_mem)

    return _merge(x2, h, (y_a, y_b, y_c, y_d, y_m), w_in[:, n_gate0:].astype(BF16),
                  row(b_in[n_gate0:]), w_up.astype(BF16), w_out.astype(BF16), row(final_g),
                  final_norm)


def kernel(x, mem, norm_g, w_in, b_in, a_conv_w, a_conv_b, a_ln_g, a_ln_b, c_conv_w, c_conv_b,
           c_f_bias, c_hn_g, d_conv_w, d_conv_b, d_wa, d_ba, d_wx, d_bx, d_lambda, mem_norm_g,
           w_mkv, w_up, w_out, final_norm_g):
    batch, seq, _ = x.shape
    n_mem = mem.shape[1]
    depth = w_in.shape[0]
    assert seq % T_CONV_A == 0 and seq % TQ_MEM == 0 and seq % ROWS_D == 0
    assert (batch * seq) % TM_IN == 0 and (batch * seq) % TM_MERGE == 0

    tri = jnp.tril(jnp.ones((BLK, BLK), F32))
    umat = jnp.concatenate([tri, jnp.ones((BLK, LANES), F32)], axis=1)
    umat = jnp.concatenate([umat, umat], axis=0).astype(BF16)
    consts = (umat, tri.astype(BF16), tri.T.astype(BF16))

    x2 = x.reshape(batch * seq, D_MODEL)
    mem2 = mem.reshape(batch * n_mem, D_MODEL)
    for l in range(depth):
        x2 = _layer(x2, mem2, batch, seq, n_mem, norm_g[l], w_in[l], b_in[l], a_conv_w[l],
                    a_conv_b[l], a_ln_g[l], a_ln_b[l], c_conv_w[l], c_conv_b[l], c_f_bias[l],
                    c_hn_g[l], d_conv_w[l], d_conv_b[l], d_wa[l], d_ba[l], d_wx[l], d_bx[l],
                    d_lambda[l], mem_norm_g[l], w_mkv[l], w_up[l], w_out[l], final_norm_g,
                    l == depth - 1, consts)
    return x2.reshape(batch, seq, D_MODEL)
```

```python
import functools

import jax
import jax.numpy as jnp
from jax import lax
from jax.experimental import pallas as pl
from jax.experimental.pallas import tpu as pltpu

F32 = jnp.float32
BF16 = jnp.bfloat16

D_MODEL = 1024
E_BRANCH = 512
N_BRANCH = 5
CONF_WIDTH = 31
H_B = 8
DH_B = E_BRANCH // H_B
H_C = 4
DH_C = E_BRANCH // H_C
SHORT_CONV = 4
NB_D = 8
BW_D = E_BRANCH // NB_D
LRU_C = 8.0
H_M = 4
DH_M = E_BRANCH // H_M
EPS = 1e-6

LOG2_E = 1.4426950408889634
LANES = 128
SUBLANES = 8
VMEM_LIMIT = 56 * 1024 * 1024

COL_A_VAL, COL_A_GLU, COL_A_Z = 0, 1, 2
COL_B_Q, COL_B_K, COL_B_V, COL_B_Z = 3, 4, 5, 6
COL_C_Q, COL_C_K, COL_C_V, COL_C_O, COL_C_Z = 7, 8, 9, 10, 11
COL_D_X, COL_D_Z = 12, 13
COL_M_Q, COL_M_Z = 14, 15
N_MAIN = 16 * E_BRANCH
IF_WIDTH = 2 * LANES

TM_IN, TN_IN = 1024, 2048
T_CONV_A = 256
ROWS_CONV_A = 32
HALO_A = 32
BLK = 128
SB_GROUP = 4
MLSTM_CHUNKS = 2
TQ_MEM = 512
TM_MERGE = 512
ROWS_D = 256


def _dot(a, b):
    return jnp.dot(a, b, preferred_element_type=F32)


def _dot_nt(a, b):
    return lax.dot_general(a, b, (((1,), (1,)), ((), ())), preferred_element_type=F32)


def _sigmoid(x):
    return 0.5 * jnp.tanh(0.5 * x) + 0.5


def _silu(x):
    return x * _sigmoid(x)


def _softplus(x):
    return jnp.maximum(x, 0.0) + jnp.log(1.0 + jnp.exp2(jnp.abs(x) * (-LOG2_E)))


def _neg_softplus(x):
    return -_softplus(x)


def _split_bf16(x, parts):
    out = []
    r = x
    for _ in range(parts):
        p = r.astype(BF16)
        out.append(p)
        r = r - p.astype(F32)
    return out


def _params(*sem):
    return pltpu.CompilerParams(dimension_semantics=sem, vmem_limit_bytes=VMEM_LIMIT)


def _const_spec(shape):
    nd = len(shape)
    return pl.BlockSpec(shape, lambda *_: (0,) * nd, pipeline_mode=pl.Buffered(1))


def _inproj_kernel(x_ref, g_ref, w_ref, b_ref, wif_ref, bif_ref, proj_ref, h_ref, if_ref, hs_ref):
    @pl.when(pl.program_id(1) == 0)
    def _():
        xf = x_ref[...]
        ms = jnp.mean(xf * xf, axis=-1, keepdims=True)
        hb = (xf * lax.rsqrt(ms + EPS) * g_ref[...]).astype(BF16)
        hs_ref[...] = hb
        h_ref[...] = hb
        if_ref[...] = _dot(hb, wif_ref[...]) + bif_ref[...]

    proj_ref[...] = (_dot(hs_ref[...], w_ref[...]) + b_ref[...]).astype(proj_ref.dtype)


def _inproj(x2, g, w_main, b_main, wif, bif):
    t = x2.shape[0]
    return pl.pallas_call(
        _inproj_kernel,
        grid=(t // TM_IN, N_MAIN // TN_IN),
        in_specs=[
            pl.BlockSpec((TM_IN, D_MODEL), lambda m, n: (m, 0)),
            _const_spec((1, D_MODEL)),
            pl.BlockSpec((D_MODEL, TN_IN), lambda m, n: (0, n)),
            pl.BlockSpec((1, TN_IN), lambda m, n: (0, n)),
            _const_spec((D_MODEL, IF_WIDTH)),
            _const_spec((1, IF_WIDTH)),
        ],
        out_specs=[
            pl.BlockSpec((TM_IN, TN_IN), lambda m, n: (m, n)),
            pl.BlockSpec((TM_IN, D_MODEL), lambda m, n: (m, 0)),
            pl.BlockSpec((TM_IN, IF_WIDTH), lambda m, n: (m, 0)),
        ],
        out_shape=[
            jax.ShapeDtypeStruct((t, N_MAIN), BF16),
            jax.ShapeDtypeStruct((t, D_MODEL), BF16),
            jax.ShapeDtypeStruct((t, IF_WIDTH), F32),
        ],
        scratch_shapes=[pltpu.VMEM((TM_IN, D_MODEL), BF16)],
        compiler_params=_params("parallel", "arbitrary"),
        name="inproj",
    )(x2, g, w_main, b_main, wif, bif)


def _conv_a_kernel(val_ref, glu_ref, z_ref, cw_ref, cb_ref, lg_ref, lb_ref, o_ref, upad, ush):
    s = pl.program_id(1)

    @pl.when(s == 0)
    def _():
        upad[0:HALO_A, :] = jnp.zeros((HALO_A, E_BRANCH), F32)

    @pl.when(s > 0)
    def _():
        upad[0:HALO_A, :] = upad[T_CONV_A:T_CONV_A + HALO_A, :]

    upad[HALO_A:HALO_A + T_CONV_A, :] = val_ref[...].astype(F32) * _sigmoid(glu_ref[...].astype(F32))

    n_rows = HALO_A + T_CONV_A
    for r in range(1, SUBLANES):
        ush[r - 1, SUBLANES:n_rows, :] = upad[SUBLANES - r:n_rows - r, :]

    for c in range(T_CONV_A // ROWS_CONV_A):
        r0 = c * ROWS_CONV_A
        acc = jnp.broadcast_to(cb_ref[...], (ROWS_CONV_A, E_BRANCH))
        for j in range(CONF_WIDTH):
            tiles_back, r = divmod(CONF_WIDTH - 1 - j, SUBLANES)
            start = HALO_A + r0 - SUBLANES * tiles_back
            if r == 0:
                src = upad[start:start + ROWS_CONV_A, :]
            else:
                src = ush[r - 1, start:start + ROWS_CONV_A, :]
            acc = acc + cw_ref[j:j + 1, :] * src
        mu = jnp.mean(acc, axis=-1, keepdims=True)
        d = acc - mu
        var = jnp.mean(d * d, axis=-1, keepdims=True)
        y = d * lax.rsqrt(var + EPS) * lg_ref[...] + lb_ref[...]
        zz = z_ref[r0:r0 + ROWS_CONV_A, :].astype(F32)
        o_ref[r0:r0 + ROWS_CONV_A, :] = (_silu(y) * _silu(zz)).astype(o_ref.dtype)


def _conv_a(proj, batch, seq, cw, cb, lg, lb):
    ns = seq // T_CONV_A

    def col(c):
        return pl.BlockSpec((T_CONV_A, E_BRANCH), lambda b, s: (b * ns + s, c))

    return pl.pallas_call(
        _conv_a_kernel,
        grid=(batch, ns),
        in_specs=[col(COL_A_VAL), col(COL_A_GLU), col(COL_A_Z),
                  _const_spec((CONF_WIDTH, E_BRANCH)), _const_spec((1, E_BRANCH)),
                  _const_spec((1, E_BRANCH)), _const_spec((1, E_BRANCH))],
        out_specs=pl.BlockSpec((T_CONV_A, E_BRANCH), lambda b, s: (b * ns + s, 0)),
        out_shape=jax.ShapeDtypeStruct((batch * seq, E_BRANCH), BF16),
        scratch_shapes=[pltpu.VMEM((HALO_A + T_CONV_A, E_BRANCH), F32),
                        pltpu.VMEM((SUBLANES - 1, HALO_A + T_CONV_A, E_BRANCH), F32)],
        compiler_params=_params("parallel", "arbitrary"),
        name="conv_a",
    )(proj, proj, proj, cw, cb, lg, lb)


def _sb_kernel(q_ref, k_ref, v_ref, zg_ref, u2_ref, o_ref,
               k2_s, v2_s, q_s, z_s, hl_s, w_s, acc_s, r_s):
    qi = pl.program_id(1)
    n_pair = H_B // 2
    n_kb = k_ref.shape[0] // BLK
    lane = lax.broadcasted_iota(jnp.int32, (BLK, LANES), 1)
    row = lax.broadcasted_iota(jnp.int32, (BLK, LANES), 0)
    strict = lane < row
    strict2 = jnp.concatenate([strict, strict], axis=1)

    @pl.when(qi == 0)
    def _():
        lane_e = lax.broadcasted_iota(jnp.int32, (BLK, E_BRANCH), 1)
        first_head = (lane_e % LANES) < DH_B

        def fill(kb, carry):
            src = pl.multiple_of(kb * BLK, BLK)
            dst = pl.multiple_of(kb * 2 * BLK, 2 * BLK)
            for src_ref, dst_ref in ((k_ref, k2_s), (v_ref, v2_s)):
                blk = src_ref[pl.ds(src, BLK), :].astype(F32)
                dst_ref[pl.ds(dst, BLK), :] = jnp.where(first_head, blk, 0.0).astype(BF16)
                dst_ref[pl.ds(dst + BLK, BLK), :] = jnp.where(first_head, 0.0, blk).astype(BF16)
            return carry

        lax.fori_loop(0, n_kb, fill, 0)

    q_s[...] = (q_ref[...].astype(F32) * (DH_B ** -0.5)).astype(BF16)

    def multi_step(kb_top, n_blk, diag):
        base = pl.multiple_of((kb_top - (n_blk - 1)) * 2 * BLK, 2 * BLK)
        for k in range(n_blk):
            rows2 = pl.ds(base + (n_blk - 1 - k) * 2 * BLK, 2 * BLK)
            for p in range(n_pair):
                lanes = slice(p * LANES, (p + 1) * LANES)
                z2 = _dot_nt(q_s[:, lanes], k2_s[rows2, lanes])
                nlk = _softplus(z2)
                if diag:
                    nlk = jnp.where(strict2, nlk, 0.0)
                z_s[k, p] = z2
                hi, lo = _split_bf16(nlk, 2)
                for hh in range(2):
                    half = slice(hh * LANES, (hh + 1) * LANES)
                    hl_s[k, 2 * p + hh] = jnp.concatenate([hi[:, half], lo[:, half]], axis=1)
        for h in range(H_B):
            p, hh = divmod(h, 2)
            half = slice(hh * LANES, (hh + 1) * LANES)
            r = None if diag else r_s[h]
            for k in range(n_blk):
                cs = _dot(hl_s[k, h], u2_ref[...])
                cum = cs[:, :LANES]
                tot = cs[:, LANES:]
                if diag:
                    w = jnp.where(strict, jnp.exp(z_s[k, p, :, half] + cum), 0.0)
                    r = tot
                else:
                    w = jnp.exp(z_s[k, p, :, half] + cum + r)
                    r = r + tot
                col = (n_blk - 1 - k) * 2 * LANES + hh * LANES
                w_s[p, :, col:col + LANES] = w.astype(BF16)
            r_s[h] = r
        for p in range(n_pair):
            lanes = slice(p * LANES, (p + 1) * LANES)
            pv = _dot(w_s[p, :, :n_blk * 2 * LANES], v2_s[pl.ds(base, n_blk * 2 * BLK), lanes])
            if diag:
                acc_s[p] = pv
            else:
                acc_s[p] += pv

    multi_step(qi, 1, True)

    n_full = qi // SB_GROUP

    def body(i, carry):
        multi_step(qi - 1 - SB_GROUP * i, SB_GROUP, False)
        return carry

    lax.fori_loop(0, n_full, body, 0)

    top = qi - 1 - SB_GROUP * n_full
    rem = qi - SB_GROUP * n_full
    size = SB_GROUP // 2
    while size >= 1:
        take = (rem & size) != 0

        @pl.when(take)
        def _(top=top, size=size):
            multi_step(top, size, False)

        top = top - jnp.where(take, size, 0)
        size //= 2

    for p in range(n_pair):
        lanes = slice(p * LANES, (p + 1) * LANES)
        zz = zg_ref[:, lanes].astype(F32)
        o_ref[:, lanes] = (acc_s[p] * _silu(zz)).astype(o_ref.dtype)


def _sb_attn(proj, batch, seq, umat):
    nq = seq // BLK
    return pl.pallas_call(
        _sb_kernel,
        grid=(batch, nq),
        in_specs=[
            pl.BlockSpec((BLK, E_BRANCH), lambda b, q: (b * nq + q, COL_B_Q)),
            pl.BlockSpec((seq, E_BRANCH), lambda b, q: (b, COL_B_K)),
            pl.BlockSpec((seq, E_BRANCH), lambda b, q: (b, COL_B_V)),
            pl.BlockSpec((BLK, E_BRANCH), lambda b, q: (b * nq + q, COL_B_Z)),
            _const_spec((2 * BLK, 2 * LANES)),
        ],
        out_specs=pl.BlockSpec((BLK, E_BRANCH), lambda b, q: (b * nq + q, 0)),
        out_shape=jax.ShapeDtypeStruct((batch * seq, E_BRANCH), BF16),
        scratch_shapes=[pltpu.VMEM((2 * seq, E_BRANCH), BF16),
                        pltpu.VMEM((2 * seq, E_BRANCH), BF16),
                        pltpu.VMEM((BLK, E_BRANCH), BF16),
                        pltpu.VMEM((SB_GROUP, H_B // 2, BLK, 2 * LANES), F32),
                        pltpu.VMEM((SB_GROUP, H_B, BLK, 2 * LANES), BF16),
                        pltpu.VMEM((H_B // 2, BLK, SB_GROUP * 2 * LANES), BF16),
                        pltpu.VMEM((H_B // 2, BLK, LANES), F32),
                        pltpu.VMEM((H_B, BLK, LANES), F32)],
        compiler_params=_params("parallel", "arbitrary"),
        name="sb_attn",
    )(proj, proj, proj, proj, umat)


def _mlstm_kernel(q_ref, k_ref, v_ref, og_ref, z_ref, ifc_ref, ir_ref, fr_ref,
                  wq_ref, wk_ref, bq_ref, bk_ref, fbc_ref, fbr_ref, g_ref, lt_ref, ut_ref,
                  sel_ref, ones_ref, mean_ref, y_ref, qpad, kpad, ct_ref, m_ref):
    step = pl.program_id(1)
    halo = SUBLANES
    rows = MLSTM_CHUNKS * BLK

    @pl.when(step == 0)
    def _():
        qpad[0:halo, :] = jnp.zeros((halo, E_BRANCH), F32)
        kpad[0:halo, :] = jnp.zeros((halo, E_BRANCH), F32)
        ct_ref[...] = jnp.zeros(ct_ref.shape, F32)
        m_ref[...] = jnp.zeros(m_ref.shape, F32)

    @pl.when(step > 0)
    def _():
        qpad[0:halo, :] = qpad[rows:rows + halo, :]
        kpad[0:halo, :] = kpad[rows:rows + halo, :]

    qpad[halo:halo + rows, :] = q_ref[...].astype(F32)
    kpad[halo:halo + rows, :] = k_ref[...].astype(F32)

    lane = lax.broadcasted_iota(jnp.int32, (BLK, LANES), 1)
    row = lax.broadcasted_iota(jnp.int32, (BLK, LANES), 0)
    causal = lane <= row
    first = halo - (SHORT_CONV - 1)
    chunks = range(MLSTM_CHUNKS)
    heads = range(H_C)
    sls = [slice(h * LANES, (h + 1) * LANES) for h in heads]

    def lane_sum(x, ones_ref):
        return _dot(jnp.concatenate(_split_bf16(x, 2), axis=1), ones_ref[...])

    qc, kc, g_row, spread = [], [], [], []
    m_prev = m_ref[...]
    for c in chunks:
        r0 = c * BLK
        q_acc = jnp.broadcast_to(bq_ref[...], (BLK, E_BRANCH))
        k_acc = jnp.broadcast_to(bk_ref[...], (BLK, E_BRANCH))
        for j in range(SHORT_CONV):
            q_acc = q_acc + wq_ref[j:j + 1, :] * qpad[r0 + first + j:r0 + first + j + BLK, :]
            k_acc = k_acc + wk_ref[j:j + 1, :] * kpad[r0 + first + j:r0 + first + j + BLK, :]
        qc.append(_silu(q_acc).astype(BF16))
        kc.append(_silu(k_acc) * (DH_C ** -0.5))

        ic = ifc_ref[r0:r0 + BLK, 0:LANES]
        fc = _neg_softplus(-(ifc_ref[r0:r0 + BLK, LANES:2 * LANES] + fbc_ref[...]))
        bc = sum(_dot(lt_ref[...], part) for part in _split_bf16(fc, 3))
        fr = _neg_softplus(-(fr_ref[0, :, r0:r0 + BLK] + fbr_ref[...]))
        br = sum(_dot(part, ut_ref[...]) for part in _split_bf16(fr, 3))
        g_row.append(ir_ref[0, :, r0:r0 + BLK] - br)

        run = ic - bc
        d = 1
        while d < BLK:
            shifted = jnp.concatenate([jnp.full((d, LANES), -jnp.inf, F32), run[:BLK - d, :]], axis=0)
            run = jnp.maximum(run, shifted)
            d *= 2
        inter = bc + m_prev
        mt = jnp.maximum(inter, bc + run)

        b_last = bc[BLK - 1:BLK, :]
        wlog = b_last - bc + ic
        m_new = jnp.maximum(b_last + m_prev, jnp.max(wlog, axis=0, keepdims=True))
        decay = jnp.exp(b_last + m_prev - m_new)
        stack = jnp.concatenate([bc - mt, jnp.exp(inter - mt), jnp.exp(-mt), jnp.exp(wlog - m_new),
                                 jnp.broadcast_to(decay, (SUBLANES, LANES))], axis=0)
        spread.append(_dot(jnp.concatenate(_split_bf16(stack, 2), axis=1), sel_ref[...]))
        m_prev = m_new
    m_ref[...] = m_prev

    def spread_of(c, h, part):
        rows_ = SUBLANES if part == 4 else BLK
        return spread[c][part * BLK:part * BLK + rows_, sls[h]]

    s_qk, qct = [], []
    for c in chunks:
        r0 = c * BLK
        s_qk.append([_dot_nt(qc[c][:, sls[h]], kc[c][:, sls[h]].astype(BF16)) for h in heads])
        qct.append([_dot(qc[c][:, sls[h]], ct_ref[h].astype(BF16)) for h in heads])
        for h in heads:
            w_st = spread_of(c, h, 3)
            vw = jnp.concatenate([v_ref[r0:r0 + BLK, sls[h]].astype(F32) * w_st, w_st], axis=1)
            k_t = kc[c][:, sls[h]].T.astype(BF16)
            dec = jnp.broadcast_to(spread_of(c, h, 4)[0:1, :], (DH_C, LANES))
            ct_ref[h] = (jnp.concatenate([dec, dec], axis=1) * ct_ref[h]
                         + _dot(k_t, vw.astype(BF16)))

    units = [(c, h) for c in chunks for h in heads]
    sc = {}
    for c, h in units:
        dlog_mt = jnp.where(causal, spread_of(c, h, 0) + g_row[c][h:h + 1, :], -jnp.inf)
        sc[c, h] = s_qk[c][h] * jnp.exp(dlog_mt)
    num_intra = {(c, h): _dot(sc[c, h].astype(BF16), v_ref[c * BLK:(c + 1) * BLK, sls[h]])
                 for c, h in units}
    den_intra = {u: lane_sum(sc[u], ones_ref) for u in units}
    hh = {}
    for c, h in units:
        w_inter = spread_of(c, h, 1)
        num = num_intra[c, h] + w_inter * qct[c][h][:, :LANES]
        den = den_intra[c, h] + w_inter * qct[c][h][:, LANES:]
        hh[c, h] = num / jnp.maximum(jnp.abs(den), spread_of(c, h, 2))
    mu = {u: lane_sum(hh[u], mean_ref) for u in units}
    dd = {u: hh[u] - mu[u] for u in units}
    var = {u: lane_sum(dd[u] * dd[u], mean_ref) for u in units}
    for c, h in units:
        rws = slice(c * BLK, (c + 1) * BLK)
        hn = dd[c, h] * lax.rsqrt(var[c, h] + EPS) * g_ref[:, sls[h]]
        y = (hn * _sigmoid(og_ref[rws, sls[h]].astype(F32))
             * _silu(z_ref[rws, sls[h]].astype(F32)))
        y_ref[rws, sls[h]] = y.astype(y_ref.dtype)


def _mlstm(proj, ifv, irow, frow, batch, seq, wq, wk, bq, bk, fbc, fbr, hn_g, ltri, utri):
    rows = MLSTM_CHUNKS * BLK
    nc = seq // rows
    k_idx = jnp.arange(2 * LANES)[:, None] % LANES
    sel = (k_idx == jnp.arange(E_BRANCH)[None, :] // LANES).astype(BF16)
    ones = jnp.ones((2 * LANES, LANES), BF16)
    mean = jnp.full((2 * LANES, LANES), 1.0 / DH_C, BF16)

    def col(c):
        return pl.BlockSpec((rows, E_BRANCH), lambda b, s: (b * nc + s, c))

    row_spec = pl.BlockSpec((1, SUBLANES, rows), lambda b, s: (b, 0, s))
    return pl.pallas_call(
        _mlstm_kernel,
        grid=(batch, nc),
        in_specs=[col(COL_C_Q), col(COL_C_K), col(COL_C_V), col(COL_C_O), col(COL_C_Z),
                  pl.BlockSpec((rows, IF_WIDTH), lambda b, s: (b * nc + s, 0)),
                  row_spec, row_spec,
                  _const_spec((SHORT_CONV, E_BRANCH)), _const_spec((SHORT_CONV, E_BRANCH)),
                  _const_spec((1, E_BRANCH)), _const_spec((1, E_BRANCH)),
                  _const_spec((1, LANES)), _const_spec((SUBLANES, LANES)),
                  _const_spec((1, E_BRANCH)),
                  _const_spec((BLK, BLK)), _const_spec((BLK, BLK)),
                  _const_spec((2 * LANES, E_BRANCH)), _const_spec((2 * LANES, LANES)),
                  _const_spec((2 * LANES, LANES))],
        out_specs=pl.BlockSpec((rows, E_BRANCH), lambda b, s: (b * nc + s, 0)),
        out_shape=jax.ShapeDtypeStruct((batch * seq, E_BRANCH), BF16),
        scratch_shapes=[pltpu.VMEM((SUBLANES + rows, E_BRANCH), F32),
                        pltpu.VMEM((SUBLANES + rows, E_BRANCH), F32),
                        pltpu.VMEM((H_C, DH_C, 2 * LANES), F32),
                        pltpu.VMEM((1, LANES), F32)],
        compiler_params=_params("parallel", "arbitrary"),
        name="mlstm",
    )(proj, proj, proj, proj, proj, ifv, irow, frow, wq, wk, bq, bk, fbc, fbr, hn_g, ltri, utri,
      sel, ones, mean)


def _rglru_kernel(x_ref, z_ref, cw_ref, cb_ref, wa_ref, ba_ref, wx_ref, bx_ref, lam_ref,
                  y_ref, xpad, a_s, h_s):
    seq = x_ref.shape[0]
    halo = SUBLANES
    xpad[0:halo, :] = jnp.zeros((halo, E_BRANCH), F32)
    xpad[halo:halo + seq, :] = x_ref[...].astype(F32)
    log_lam = _neg_softplus(-lam_ref[...])

    first = halo - (SHORT_CONV - 1)
    for c in range(seq // ROWS_D):
        r0 = c * ROWS_D
        xc = jnp.broadcast_to(cb_ref[...], (ROWS_D, E_BRANCH))
        for j in range(SHORT_CONV):
            xc = xc + cw_ref[j:j + 1, :] * xpad[r0 + first + j:r0 + first + j + ROWS_D, :]
        xb = xc.astype(BF16)
        r = _sigmoid(_dot(xb, wa_ref[...]) + ba_ref[...])
        i = _sigmoid(_dot(xb, wx_ref[...]) + bx_ref[...])
        log_a = LRU_C * r * log_lam
        a = jnp.exp(log_a)
        u = jnp.sqrt(-jnp.tanh(log_a) * (a * a + 1.0)) * (i * xc)
        a_s[r0:r0 + ROWS_D, :] = a
        h_s[r0:r0 + ROWS_D, :] = u

    sub = lax.broadcasted_iota(jnp.int32, (SUBLANES, E_BRANCH), 0)
    rows_out = 2 * SUBLANES

    def scan_body(g, carry):
        r0 = pl.multiple_of(g * rows_out, rows_out)
        tiles = []
        for part in range(rows_out // SUBLANES):
            rows = pl.ds(r0 + part * SUBLANES, SUBLANES)
            a = a_s[rows, :]
            u = h_s[rows, :]
            for d in (1, 2, 4):
                keep = sub >= d
                u_prev = jnp.where(keep, pltpu.roll(u, d, 0), 0.0)
                a_prev = jnp.where(keep, pltpu.roll(a, d, 0), 1.0)
                u = u + a * u_prev
                a = a * a_prev
            h = u + a * carry
            carry = jnp.broadcast_to(h[SUBLANES - 1:SUBLANES, :], (SUBLANES, E_BRANCH))
            tiles.append(h)
        hs = jnp.concatenate(tiles, axis=0)
        zz = z_ref[pl.ds(r0, rows_out), :].astype(F32)
        y_ref[pl.ds(r0, rows_out), :] = (hs * _silu(zz)).astype(y_ref.dtype)
        return carry

    lax.fori_loop(0, seq // rows_out, scan_body, jnp.zeros((SUBLANES, E_BRANCH), F32))


def _rglru(proj, batch, seq, cw, cb, wa, ba, wx, bx, lam):
    return pl.pallas_call(
        _rglru_kernel,
        grid=(batch,),
        in_specs=[pl.BlockSpec((seq, E_BRANCH), lambda b: (b, COL_D_X)),
                  pl.BlockSpec((seq, E_BRANCH), lambda b: (b, COL_D_Z)),
                  _const_spec((SHORT_CONV, E_BRANCH)), _const_spec((1, E_BRANCH)),
                  _const_spec((E_BRANCH, E_BRANCH)), _const_spec((1, E_BRANCH)),
                  _const_spec((E_BRANCH, E_BRANCH)), _const_spec((1, E_BRANCH)),
                  _const_spec((1, E_BRANCH))],
        out_specs=pl.BlockSpec((seq, E_BRANCH), lambda b: (b, 0)),
        out_shape=jax.ShapeDtypeStruct((batch * seq, E_BRANCH), BF16),
        scratch_shapes=[pltpu.VMEM((SUBLANES + seq, E_BRANCH), F32),
                        pltpu.VMEM((seq, E_BRANCH), F32),
                        pltpu.VMEM((seq, E_BRANCH), F32)],
        compiler_params=_params("parallel"),
        name="rglru",
    )(proj, proj, cw, cb, wa, ba, wx, bx, lam)


def _memkv_kernel(mem_ref, g_ref, w_ref, kv_ref):
    xf = mem_ref[...]
    ms = jnp.mean(xf * xf, axis=-1, keepdims=True)
    mn = (xf * lax.rsqrt(ms + EPS) * g_ref[...]).astype(BF16)
    kv_ref[...] = _dot(mn, w_ref[...]).astype(kv_ref.dtype)


def _memkv(mem2, batch, n_mem, g, w):
    return pl.pallas_call(
        _memkv_kernel,
        grid=(batch,),
        in_specs=[pl.BlockSpec((n_mem, D_MODEL), lambda b: (b, 0)),
                  _const_spec((1, D_MODEL)),
                  _const_spec((D_MODEL, 2 * E_BRANCH))],
        out_specs=pl.BlockSpec((n_mem, 2 * E_BRANCH), lambda b: (b, 0)),
        out_shape=jax.ShapeDtypeStruct((batch * n_mem, 2 * E_BRANCH), BF16),
        compiler_params=_params("parallel"),
        name="memkv",
    )(mem2, g, w)


def _memattn_kernel(q_ref, z_ref, kv_ref, y_ref):
    scale = DH_M ** -0.5
    for h in range(H_M):
        sl = slice(h * LANES, (h + 1) * LANES)
        kh = kv_ref[:, sl]
        vh = kv_ref[:, E_BRANCH + h * LANES:E_BRANCH + (h + 1) * LANES]
        s = _dot_nt(q_ref[:, sl], kh) * scale
        p = jnp.exp(s - jnp.max(s, axis=-1, keepdims=True))
        o = _dot(p.astype(BF16), vh) / jnp.sum(p, axis=-1, keepdims=True)
        y_ref[:, sl] = (o * _silu(z_ref[:, sl].astype(F32))).astype(y_ref.dtype)


def _memattn(proj, kv, batch, seq, n_mem):
    nq = seq // TQ_MEM
    return pl.pallas_call(
        _memattn_kernel,
        grid=(batch, nq),
        in_specs=[pl.BlockSpec((TQ_MEM, E_BRANCH), lambda b, q: (b * nq + q, COL_M_Q)),
                  pl.BlockSpec((TQ_MEM, E_BRANCH), lambda b, q: (b * nq + q, COL_M_Z)),
                  pl.BlockSpec((n_mem, 2 * E_BRANCH), lambda b, q: (b, 0))],
        out_specs=pl.BlockSpec((TQ_MEM, E_BRANCH), lambda b, q: (b * nq + q, 0)),
        out_shape=jax.ShapeDtypeStruct((batch * seq, E_BRANCH), BF16),
        compiler_params=_params("parallel", "parallel"),
        name="memattn",
    )(proj, proj, kv)


def _merge_kernel(x_ref, h_ref, ya_ref, yb_ref, yc_ref, yd_ref, ym_ref, wg_ref, bg_ref,
                  wup_ref, wout_ref, fg_ref, o_ref, *, final_norm):
    hb = h_ref[...]
    merged = None
    for n, y_ref in enumerate((ya_ref, yb_ref, yc_ref, yd_ref, ym_ref)):
        cols = slice(n * D_MODEL, (n + 1) * D_MODEL)
        gate = _sigmoid(_dot(hb, wg_ref[:, cols]) + bg_ref[:, cols])
        term = gate * _dot(y_ref[...], wup_ref[n])
        merged = term if merged is None else merged + term
    out = x_ref[...] + _dot(merged.astype(BF16), wout_ref[...])
    if final_norm:
        ms = jnp.mean(out * out, axis=-1, keepdims=True)
        out = out * lax.rsqrt(ms + EPS) * fg_ref[...]
    o_ref[...] = out


def _merge(x2, h, ys, wg, bg, wup, wout, fg, final_norm):
    t = x2.shape[0]
    row = lambda width: pl.BlockSpec((TM_MERGE, width), lambda m: (m, 0))
    return pl.pallas_call(
        functools.partial(_merge_kernel, final_norm=final_norm),
        grid=(t // TM_MERGE,),
        in_specs=[row(D_MODEL), row(D_MODEL)] + [row(E_BRANCH)] * N_BRANCH + [
            _const_spec((D_MODEL, N_BRANCH * D_MODEL)), _const_spec((1, N_BRANCH * D_MODEL)),
            _const_spec((N_BRANCH, E_BRANCH, D_MODEL)), _const_spec((D_MODEL, D_MODEL)),
            _const_spec((1, D_MODEL))],
        out_specs=row(D_MODEL),
        out_shape=jax.ShapeDtypeStruct((t, D_MODEL), F32),
        compiler_params=_params("parallel"),
        name="merge_final" if final_norm else "merge",
    )(x2, h, *ys, wg, bg, wup, wout, fg)


def _block_diag(w):
    nb, bw, _ = w.shape
    eye = jnp.eye(nb, dtype=w.dtype)
    return (eye[:, None, :, None] * w[:, :, None, :]).reshape(nb * bw, nb * bw)


def _layer(x2, mem2, batch, seq, n_mem, norm_g, w_in, b_in, a_conv_w, a_conv_b, a_ln_g, a_ln_b,
           c_conv_w, c_conv_b, c_f_bias, c_hn_g, d_conv_w, d_conv_b, d_wa, d_ba, d_wx, d_bx,
           d_lambda, mem_norm_g, w_mkv, w_up, w_out, final_g, final_norm, consts):
    umat, ltri, utri = consts
    row = lambda v: v.reshape(1, -1)
    n_pre = 10 * E_BRANCH
    n_if = 2 * H_C
    n_gate0 = N_MAIN + n_if
    w_main = jnp.concatenate([w_in[:, :n_pre], w_in[:, n_pre + n_if:n_gate0]], axis=1).astype(BF16)
    b_main = row(jnp.concatenate([b_in[:n_pre], b_in[n_pre + n_if:n_gate0]]))
    wif = jnp.zeros((D_MODEL, IF_WIDTH), F32)
    wif = wif.at[:, 0:H_C].set(w_in[:, n_pre:n_pre + H_C])
    wif = wif.at[:, LANES:LANES + H_C].set(w_in[:, n_pre + H_C:n_pre + n_if]).astype(BF16)
    bif = jnp.zeros((1, IF_WIDTH), F32)
    bif = bif.at[0, 0:H_C].set(b_in[n_pre:n_pre + H_C])
    bif = bif.at[0, LANES:LANES + H_C].set(b_in[n_pre + H_C:n_pre + n_if])

    proj, h, ifv = _inproj(x2, row(norm_g), w_main, b_main, wif, bif)

    y_a = _conv_a(proj, batch, seq, a_conv_w, row(a_conv_b), row(a_ln_g), row(a_ln_b))
    y_b = _sb_attn(proj, batch, seq, umat)

    pad_rows = ((0, 0), (0, SUBLANES - H_C), (0, 0))
    irow = jnp.pad(ifv[:, 0:H_C].reshape(batch, seq, H_C).transpose(0, 2, 1), pad_rows)
    frow = jnp.pad(ifv[:, LANES:LANES + H_C].reshape(batch, seq, H_C).transpose(0, 2, 1), pad_rows)
    fbc = jnp.zeros((1, LANES), F32).at[0, 0:H_C].set(c_f_bias)
    fbr = jnp.zeros((SUBLANES, LANES), F32).at[0:H_C, :].set(
        jnp.broadcast_to(c_f_bias[:, None], (H_C, LANES)))
    y_c = _mlstm(proj, ifv, irow, frow, batch, seq,
                 c_conv_w[:, :E_BRANCH], c_conv_w[:, E_BRANCH:],
                 row(c_conv_b[:E_BRANCH]), row(c_conv_b[E_BRANCH:]),
                 fbc, fbr, row(c_hn_g), ltri, utri)

    y_d = _rglru(proj, batch, seq, d_conv_w, row(d_conv_b),
                 _block_diag(d_wa).astype(BF16), row(d_ba),
                 _block_diag(d_wx).astype(BF16), row(d_bx), row(d_lambda))

    kv = _memkv(mem2, batch, n_mem, row(mem_norm_g), w_mkv.astype(BF16))
    y_m = _memattn(proj, kv, batch, seq, n_mem)

    return _merge(x2, h, (y_a, y_b, y_c, y_d, y_m), w_in[:, n_gate0:].astype(BF16),
                  row(b_in[n_gate0:]), w_up.astype(BF16), w_out.astype(BF16), row(final_g),
                  final_norm)


def kernel(x, mem, norm_g, w_in, b_in, a_conv_w, a_conv_b, a_ln_g, a_ln_b, c_conv_w, c_conv_b,
           c_f_bias, c_hn_g, d_conv_w, d_conv_b, d_wa, d_ba, d_wx, d_bx, d_lambda, mem_norm_g,
           w_mkv, w_up, w_out, final_norm_g):
    batch, seq, _ = x.shape
    n_mem = mem.shape[1]
    depth = w_in.shape[0]
    assert seq % T_CONV_A == 0 and seq % TQ_MEM == 0 and seq % ROWS_D == 0
    assert seq % (MLSTM_CHUNKS * BLK) == 0
    assert (batch * seq) % TM_IN == 0 and (batch * seq) % TM_MERGE == 0

    tri = jnp.tril(jnp.ones((BLK, BLK), F32))
    umat = -jnp.concatenate([tri, jnp.ones((BLK, LANES), F32)], axis=1)
    umat = jnp.concatenate([umat, umat], axis=0).astype(BF16)
    consts = (umat, tri.astype(BF16), tri.T.astype(BF16))

    x2 = x.reshape(batch * seq, D_MODEL)
    mem2 = mem.reshape(batch * n_mem, D_MODEL)
    for l in range(depth):
        x2 = _layer(x2, mem2, batch, seq, n_mem, norm_g[l], w_in[l], b_in[l], a_conv_w[l],
                    a_conv_b[l], a_ln_g[l], a_ln_b[l], c_conv_w[l], c_conv_b[l], c_f_bias[l],
                    c_hn_g[l], d_conv_w[l], d_conv_b[l], d_wa[l], d_ba[l], d_wx[l], d_bx[l],
                    d_lambda[l], mem_norm_g[l], w_mkv[l], w_up[l], w_out[l], final_norm_g,
                    l == depth - 1, consts)
    return x2.reshape(batch, seq, D_MODEL)
```

```python
import functools

import jax
import jax.numpy as jnp
from jax import lax
from jax.experimental import pallas as pl
from jax.experimental.pallas import tpu as pltpu

F32 = jnp.float32
BF16 = jnp.bfloat16

D_MODEL = 1024
E_BRANCH = 512
N_BRANCH = 5
CONF_WIDTH = 31
H_B = 8
DH_B = E_BRANCH // H_B
H_C = 4
DH_C = E_BRANCH // H_C
SHORT_CONV = 4
NB_D = 8
BW_D = E_BRANCH // NB_D
LRU_C = 8.0
H_M = 4
DH_M = E_BRANCH // H_M
EPS = 1e-6

LOG2_E = 1.4426950408889634
LANES = 128
SUBLANES = 8
VMEM_LIMIT = 56 * 1024 * 1024

COL_A_VAL, COL_A_GLU, COL_A_Z = 0, 1, 2
COL_B_Q, COL_B_K, COL_B_V, COL_B_Z = 3, 4, 5, 6
COL_C_Q, COL_C_K, COL_C_V, COL_C_O, COL_C_Z = 7, 8, 9, 10, 11
COL_D_X, COL_D_Z = 12, 13
COL_M_Q, COL_M_Z = 14, 15
N_MAIN = 16 * E_BRANCH
IF_WIDTH = 2 * LANES

TM_IN, TN_IN = 1024, 2048
T_CONV_A = 256
ROWS_CONV_A = 32
HALO_A = 32
BLK = 128
SB_Q = 256
SB_GROUP = 4
MLSTM_CHUNKS = 2
TQ_MEM = 512
TM_MERGE = 512
ROWS_D = 256


def _dot(a, b):
    return jnp.dot(a, b, preferred_element_type=F32)


def _dot_nt(a, b):
    return lax.dot_general(a, b, (((1,), (1,)), ((), ())), preferred_element_type=F32)


def _sigmoid(x):
    return 0.5 * jnp.tanh(0.5 * x) + 0.5


def _silu(x):
    return x * _sigmoid(x)


def _softplus(x):
    return jnp.maximum(x, 0.0) + jnp.log(1.0 + jnp.exp2(jnp.abs(x) * (-LOG2_E)))


def _neg_softplus(x):
    return -_softplus(x)


def _split_bf16(x, parts):
    out = []
    r = x
    for _ in range(parts):
        p = r.astype(BF16)
        out.append(p)
        r = r - p.astype(F32)
    return out


def _params(*sem):
    return pltpu.CompilerParams(dimension_semantics=sem, vmem_limit_bytes=VMEM_LIMIT)


def _const_spec(shape):
    nd = len(shape)
    return pl.BlockSpec(shape, lambda *_: (0,) * nd, pipeline_mode=pl.Buffered(1))


def _inproj_kernel(x_ref, g_ref, w_ref, b_ref, wif_ref, bif_ref, proj_ref, h_ref, if_ref, hs_ref):
    @pl.when(pl.program_id(1) == 0)
    def _():
        xf = x_ref[...]
        ms = jnp.mean(xf * xf, axis=-1, keepdims=True)
        hb = (xf * lax.rsqrt(ms + EPS) * g_ref[...]).astype(BF16)
        hs_ref[...] = hb
        h_ref[...] = hb
        if_ref[...] = _dot(hb, wif_ref[...]) + bif_ref[...]

    proj_ref[...] = (_dot(hs_ref[...], w_ref[...]) + b_ref[...]).astype(proj_ref.dtype)


def _inproj(x2, g, w_main, b_main, wif, bif):
    t = x2.shape[0]
    return pl.pallas_call(
        _inproj_kernel,
        grid=(t // TM_IN, N_MAIN // TN_IN),
        in_specs=[
            pl.BlockSpec((TM_IN, D_MODEL), lambda m, n: (m, 0)),
            _const_spec((1, D_MODEL)),
            pl.BlockSpec((D_MODEL, TN_IN), lambda m, n: (0, n)),
            pl.BlockSpec((1, TN_IN), lambda m, n: (0, n)),
            _const_spec((D_MODEL, IF_WIDTH)),
            _const_spec((1, IF_WIDTH)),
        ],
        out_specs=[
            pl.BlockSpec((TM_IN, TN_IN), lambda m, n: (m, n)),
            pl.BlockSpec((TM_IN, D_MODEL), lambda m, n: (m, 0)),
            pl.BlockSpec((TM_IN, IF_WIDTH), lambda m, n: (m, 0)),
        ],
        out_shape=[
            jax.ShapeDtypeStruct((t, N_MAIN), BF16),
            jax.ShapeDtypeStruct((t, D_MODEL), BF16),
            jax.ShapeDtypeStruct((t, IF_WIDTH), F32),
        ],
        scratch_shapes=[pltpu.VMEM((TM_IN, D_MODEL), BF16)],
        compiler_params=_params("parallel", "arbitrary"),
        name="inproj",
    )(x2, g, w_main, b_main, wif, bif)


def _conv_a_kernel(val_ref, glu_ref, z_ref, cw_ref, cb_ref, lg_ref, lb_ref, o_ref, upad, ush):
    s = pl.program_id(1)

    @pl.when(s == 0)
    def _():
        upad[0:HALO_A, :] = jnp.zeros((HALO_A, E_BRANCH), F32)

    @pl.when(s > 0)
    def _():
        upad[0:HALO_A, :] = upad[T_CONV_A:T_CONV_A + HALO_A, :]

    upad[HALO_A:HALO_A + T_CONV_A, :] = val_ref[...].astype(F32) * _sigmoid(glu_ref[...].astype(F32))

    n_rows = HALO_A + T_CONV_A
    for r in range(1, SUBLANES):
        ush[r - 1, SUBLANES:n_rows, :] = upad[SUBLANES - r:n_rows - r, :]

    for c in range(T_CONV_A // ROWS_CONV_A):
        r0 = c * ROWS_CONV_A
        acc = jnp.broadcast_to(cb_ref[...], (ROWS_CONV_A, E_BRANCH))
        for j in range(CONF_WIDTH):
            tiles_back, r = divmod(CONF_WIDTH - 1 - j, SUBLANES)
            start = HALO_A + r0 - SUBLANES * tiles_back
            if r == 0:
                src = upad[start:start + ROWS_CONV_A, :]
            else:
                src = ush[r - 1, start:start + ROWS_CONV_A, :]
            acc = acc + cw_ref[j:j + 1, :] * src
        mu = jnp.mean(acc, axis=-1, keepdims=True)
        d = acc - mu
        var = jnp.mean(d * d, axis=-1, keepdims=True)
        y = d * lax.rsqrt(var + EPS) * lg_ref[...] + lb_ref[...]
        zz = z_ref[r0:r0 + ROWS_CONV_A, :].astype(F32)
        o_ref[r0:r0 + ROWS_CONV_A, :] = (_silu(y) * _silu(zz)).astype(o_ref.dtype)


def _conv_a(proj, batch, seq, cw, cb, lg, lb):
    ns = seq // T_CONV_A

    def col(c):
        return pl.BlockSpec((T_CONV_A, E_BRANCH), lambda b, s: (b * ns + s, c))

    return pl.pallas_call(
        _conv_a_kernel,
        grid=(batch, ns),
        in_specs=[col(COL_A_VAL), col(COL_A_GLU), col(COL_A_Z),
                  _const_spec((CONF_WIDTH, E_BRANCH)), _const_spec((1, E_BRANCH)),
                  _const_spec((1, E_BRANCH)), _const_spec((1, E_BRANCH))],
        out_specs=pl.BlockSpec((T_CONV_A, E_BRANCH), lambda b, s: (b * ns + s, 0)),
        out_shape=jax.ShapeDtypeStruct((batch * seq, E_BRANCH), BF16),
        scratch_shapes=[pltpu.VMEM((HALO_A + T_CONV_A, E_BRANCH), F32),
                        pltpu.VMEM((SUBLANES - 1, HALO_A + T_CONV_A, E_BRANCH), F32)],
        compiler_params=_params("parallel", "arbitrary"),
        name="conv_a",
    )(proj, proj, proj, cw, cb, lg, lb)


def _sb_kernel(q_ref, k_ref, v_ref, zg_ref, u2_ref, o_ref,
               k2_s, v2_s, q_s, z_s, hl_s, w_s, acc_s, r_s):
    qi = pl.program_id(1)
    n_pair = H_B // 2
    n_kb = k_ref.shape[0] // BLK
    kb_per_q = SB_Q // BLK
    lane = lax.broadcasted_iota(jnp.int32, (SB_Q, LANES), 1)
    row = lax.broadcasted_iota(jnp.int32, (SB_Q, LANES), 0)
    strict = [(j * BLK + lane) < row for j in range(kb_per_q)]

    @pl.when(qi == 0)
    def _():
        lane_e = lax.broadcasted_iota(jnp.int32, (BLK, E_BRANCH), 1)
        first_head = (lane_e % LANES) < DH_B

        def fill(kb, carry):
            src = pl.multiple_of(kb * BLK, BLK)
            dst = pl.multiple_of(kb * 2 * BLK, 2 * BLK)
            for src_ref, dst_ref in ((k_ref, k2_s), (v_ref, v2_s)):
                blk = src_ref[pl.ds(src, BLK), :].astype(F32)
                dst_ref[pl.ds(dst, BLK), :] = jnp.where(first_head, blk, 0.0).astype(BF16)
                dst_ref[pl.ds(dst + BLK, BLK), :] = jnp.where(first_head, 0.0, blk).astype(BF16)
            return carry

        lax.fori_loop(0, n_kb, fill, 0)

    q_s[...] = (q_ref[...].astype(F32) * (DH_B ** -0.5)).astype(BF16)

    def multi_step(kb_top, n_blk, diag):
        base = pl.multiple_of((kb_top - (n_blk - 1)) * 2 * BLK, 2 * BLK)
        for k in range(n_blk):
            rows2 = pl.ds(base + (n_blk - 1 - k) * 2 * BLK, 2 * BLK)
            for p in range(n_pair):
                lanes = slice(p * LANES, (p + 1) * LANES)
                z2 = _dot_nt(q_s[:, lanes], k2_s[rows2, lanes])
                nlk = _softplus(z2)
                if diag:
                    mask = strict[n_blk - 1 - k]
                    nlk = jnp.where(jnp.concatenate([mask, mask], axis=1), nlk, 0.0)
                z_s[k, p] = z2
                hi, lo = _split_bf16(nlk, 2)
                for hh in range(2):
                    half = slice(hh * LANES, (hh + 1) * LANES)
                    hl_s[k, 2 * p + hh] = jnp.concatenate([hi[:, half], lo[:, half]], axis=1)
        for h in range(H_B):
            p, hh = divmod(h, 2)
            half = slice(hh * LANES, (hh + 1) * LANES)
            r = None if diag else r_s[h]
            for k in range(n_blk):
                cs = _dot(hl_s[k, h], u2_ref[...])
                cum = cs[:, :LANES]
                tot = cs[:, LANES:]
                log_w = z_s[k, p, :, half] + cum
                if r is not None:
                    log_w = log_w + r
                w = jnp.exp(log_w)
                if diag:
                    w = jnp.where(strict[n_blk - 1 - k], w, 0.0)
                r = tot if r is None else r + tot
                col = (n_blk - 1 - k) * 2 * LANES + hh * LANES
                w_s[p, :, col:col + LANES] = w.astype(BF16)
            r_s[h] = r
        for p in range(n_pair):
            lanes = slice(p * LANES, (p + 1) * LANES)
            pv = _dot(w_s[p, :, :n_blk * 2 * LANES], v2_s[pl.ds(base, n_blk * 2 * BLK), lanes])
            if diag:
                acc_s[p] = pv
            else:
                acc_s[p] += pv

    n_off = qi * kb_per_q
    multi_step(n_off + kb_per_q - 1, kb_per_q, True)

    n_full = n_off // SB_GROUP

    def body(i, carry):
        multi_step(n_off - 1 - SB_GROUP * i, SB_GROUP, False)
        return carry

    lax.fori_loop(0, n_full, body, 0)

    top = n_off - 1 - SB_GROUP * n_full
    rem = n_off - SB_GROUP * n_full
    size = SB_GROUP // 2
    while size >= 1:
        take = (rem & size) != 0

        @pl.when(take)
        def _(top=top, size=size):
            multi_step(top, size, False)

        top = top - jnp.where(take, size, 0)
        size //= 2

    for p in range(n_pair):
        lanes = slice(p * LANES, (p + 1) * LANES)
        zz = zg_ref[:, lanes].astype(F32)
        o_ref[:, lanes] = (acc_s[p] * _silu(zz)).astype(o_ref.dtype)


def _sb_attn(proj, batch, seq, umat):
    nq = seq // SB_Q
    return pl.pallas_call(
        _sb_kernel,
        grid=(batch, nq),
        in_specs=[
            pl.BlockSpec((SB_Q, E_BRANCH), lambda b, q: (b * nq + q, COL_B_Q)),
            pl.BlockSpec((seq, E_BRANCH), lambda b, q: (b, COL_B_K)),
            pl.BlockSpec((seq, E_BRANCH), lambda b, q: (b, COL_B_V)),
            pl.BlockSpec((SB_Q, E_BRANCH), lambda b, q: (b * nq + q, COL_B_Z)),
            _const_spec((2 * BLK, 2 * LANES)),
        ],
        out_specs=pl.BlockSpec((SB_Q, E_BRANCH), lambda b, q: (b * nq + q, 0)),
        out_shape=jax.ShapeDtypeStruct((batch * seq, E_BRANCH), BF16),
        scratch_shapes=[pltpu.VMEM((2 * seq, E_BRANCH), BF16),
                        pltpu.VMEM((2 * seq, E_BRANCH), BF16),
                        pltpu.VMEM((SB_Q, E_BRANCH), BF16),
                        pltpu.VMEM((SB_GROUP, H_B // 2, SB_Q, 2 * LANES), F32),
                        pltpu.VMEM((SB_GROUP, H_B, SB_Q, 2 * LANES), BF16),
                        pltpu.VMEM((H_B // 2, SB_Q, SB_GROUP * 2 * LANES), BF16),
                        pltpu.VMEM((H_B // 2, SB_Q, LANES), F32),
                        pltpu.VMEM((H_B, SB_Q, LANES), F32)],
        compiler_params=_params("parallel", "arbitrary"),
        name="sb_attn",
    )(proj, proj, proj, proj, umat)


def _mlstm_kernel(q_ref, k_ref, v_ref, og_ref, z_ref, ifc_ref, ir_ref, fr_ref,
                  wq_ref, wk_ref, bq_ref, bk_ref, fbc_ref, fbr_ref, g_ref, lt_ref, ut_ref,
                  sel_ref, ones_ref, mean_ref, y_ref, qpad, kpad, ct_ref, m_ref):
    step = pl.program_id(1)
    halo = SUBLANES
    rows = MLSTM_CHUNKS * BLK

    @pl.when(step == 0)
    def _():
        qpad[0:halo, :] = jnp.zeros((halo, E_BRANCH), F32)
        kpad[0:halo, :] = jnp.zeros((halo, E_BRANCH), F32)
        ct_ref[...] = jnp.zeros(ct_ref.shape, F32)
        m_ref[...] = jnp.zeros(m_ref.shape, F32)

    @pl.when(step > 0)
    def _():
        qpad[0:halo, :] = qpad[rows:rows + halo, :]
        kpad[0:halo, :] = kpad[rows:rows + halo, :]

    qpad[halo:halo + rows, :] = q_ref[...].astype(F32)
    kpad[halo:halo + rows, :] = k_ref[...].astype(F32)

    lane = lax.broadcasted_iota(jnp.int32, (BLK, LANES), 1)
    row = lax.broadcasted_iota(jnp.int32, (BLK, LANES), 0)
    causal = lane <= row
    first = halo - (SHORT_CONV - 1)
    chunks = range(MLSTM_CHUNKS)
    heads = range(H_C)
    sls = [slice(h * LANES, (h + 1) * LANES) for h in heads]

    def lane_sum(x, ones_ref):
        return _dot(jnp.concatenate(_split_bf16(x, 2), axis=1), ones_ref[...])

    qc, kc, g_row, spread = [], [], [], []
    m_prev = m_ref[...]
    for c in chunks:
        r0 = c * BLK
        q_acc = jnp.broadcast_to(bq_ref[...], (BLK, E_BRANCH))
        k_acc = jnp.broadcast_to(bk_ref[...], (BLK, E_BRANCH))
        for j in range(SHORT_CONV):
            q_acc = q_acc + wq_ref[j:j + 1, :] * qpad[r0 + first + j:r0 + first + j + BLK, :]
            k_acc = k_acc + wk_ref[j:j + 1, :] * kpad[r0 + first + j:r0 + first + j + BLK, :]
        qc.append(_silu(q_acc).astype(BF16))
        kc.append(_silu(k_acc) * (DH_C ** -0.5))

        ic = ifc_ref[r0:r0 + BLK, 0:LANES]
        fc = _neg_softplus(-(ifc_ref[r0:r0 + BLK, LANES:2 * LANES] + fbc_ref[...]))
        bc = sum(_dot(lt_ref[...], part) for part in _split_bf16(fc, 3))
        fr = _neg_softplus(-(fr_ref[0, :, r0:r0 + BLK] + fbr_ref[...]))
        br = sum(_dot(part, ut_ref[...]) for part in _split_bf16(fr, 3))
        g_row.append(ir_ref[0, :, r0:r0 + BLK] - br)

        run = ic - bc
        d = 1
        while d < BLK:
            shifted = jnp.concatenate([jnp.full((d, LANES), -jnp.inf, F32), run[:BLK - d, :]], axis=0)
            run = jnp.maximum(run, shifted)
            d *= 2
        inter = bc + m_prev
        mt = jnp.maximum(inter, bc + run)

        b_last = bc[BLK - 1:BLK, :]
        wlog = b_last - bc + ic
        m_new = jnp.maximum(b_last + m_prev, jnp.max(wlog, axis=0, keepdims=True))
        decay = jnp.exp(b_last + m_prev - m_new)
        stack = jnp.concatenate([bc - mt, jnp.exp(inter - mt), jnp.exp(-mt), jnp.exp(wlog - m_new),
                                 jnp.broadcast_to(decay, (SUBLANES, LANES))], axis=0)
        spread.append(_dot(jnp.concatenate(_split_bf16(stack, 2), axis=1), sel_ref[...]))
        m_prev = m_new
    m_ref[...] = m_prev

    def spread_of(c, h, part):
        rows_ = SUBLANES if part == 4 else BLK
        return spread[c][part * BLK:part * BLK + rows_, sls[h]]

    s_qk, qct = [], []
    for c in chunks:
        r0 = c * BLK
        s_qk.append([_dot_nt(qc[c][:, sls[h]], kc[c][:, sls[h]].astype(BF16)) for h in heads])
        qct.append([_dot(qc[c][:, sls[h]], ct_ref[h].astype(BF16)) for h in heads])
        for h in heads:
            w_st = spread_of(c, h, 3)
            vw = jnp.concatenate([v_ref[r0:r0 + BLK, sls[h]].astype(F32) * w_st, w_st], axis=1)
            k_t = kc[c][:, sls[h]].T.astype(BF16)
            dec = jnp.broadcast_to(spread_of(c, h, 4)[0:1, :], (DH_C, LANES))
            ct_ref[h] = (jnp.concatenate([dec, dec], axis=1) * ct_ref[h]
                         + _dot(k_t, vw.astype(BF16)))

    units = [(c, h) for c in chunks for h in heads]
    sc = {}
    for c, h in units:
        dlog_mt = jnp.where(causal, spread_of(c, h, 0) + g_row[c][h:h + 1, :], -jnp.inf)
        sc[c, h] = s_qk[c][h] * jnp.exp(dlog_mt)
    num_intra = {(c, h): _dot(sc[c, h].astype(BF16), v_ref[c * BLK:(c + 1) * BLK, sls[h]])
                 for c, h in units}
    den_intra = {u: lane_sum(sc[u], ones_ref) for u in units}
    hh = {}
    for c, h in units:
        w_inter = spread_of(c, h, 1)
        num = num_intra[c, h] + w_inter * qct[c][h][:, :LANES]
        den = den_intra[c, h] + w_inter * qct[c][h][:, LANES:]
        hh[c, h] = num / jnp.maximum(jnp.abs(den), spread_of(c, h, 2))
    mu = {u: lane_sum(hh[u], mean_ref) for u in units}
    dd = {u: hh[u] - mu[u] for u in units}
    var = {u: lane_sum(dd[u] * dd[u], mean_ref) for u in units}
    for c, h in units:
        rws = slice(c * BLK, (c + 1) * BLK)
        hn = dd[c, h] * lax.rsqrt(var[c, h] + EPS) * g_ref[:, sls[h]]
        y = (hn * _sigmoid(og_ref[rws, sls[h]].astype(F32))
             * _silu(z_ref[rws, sls[h]].astype(F32)))
        y_ref[rws, sls[h]] = y.astype(y_ref.dtype)


def _mlstm(proj, ifv, irow, frow, batch, seq, wq, wk, bq, bk, fbc, fbr, hn_g, ltri, utri):
    rows = MLSTM_CHUNKS * BLK
    nc = seq // rows
    k_idx = jnp.arange(2 * LANES)[:, None] % LANES
    sel = (k_idx == jnp.arange(E_BRANCH)[None, :] // LANES).astype(BF16)
    ones = jnp.ones((2 * LANES, LANES), BF16)
    mean = jnp.full((2 * LANES, LANES), 1.0 / DH_C, BF16)

    def col(c):
        return pl.BlockSpec((rows, E_BRANCH), lambda b, s: (b * nc + s, c))

    row_spec = pl.BlockSpec((1, SUBLANES, rows), lambda b, s: (b, 0, s))
    return pl.pallas_call(
        _mlstm_kernel,
        grid=(batch, nc),
        in_specs=[col(COL_C_Q), col(COL_C_K), col(COL_C_V), col(COL_C_O), col(COL_C_Z),
                  pl.BlockSpec((rows, IF_WIDTH), lambda b, s: (b * nc + s, 0)),
                  row_spec, row_spec,
                  _const_spec((SHORT_CONV, E_BRANCH)), _const_spec((SHORT_CONV, E_BRANCH)),
                  _const_spec((1, E_BRANCH)), _const_spec((1, E_BRANCH)),
                  _const_spec((1, LANES)), _const_spec((SUBLANES, LANES)),
                  _const_spec((1, E_BRANCH)),
                  _const_spec((BLK, BLK)), _const_spec((BLK, BLK)),
                  _const_spec((2 * LANES, E_BRANCH)), _const_spec((2 * LANES, LANES)),
                  _const_spec((2 * LANES, LANES))],
        out_specs=pl.BlockSpec((rows, E_BRANCH), lambda b, s: (b * nc + s, 0)),
        out_shape=jax.ShapeDtypeStruct((batch * seq, E_BRANCH), BF16),
        scratch_shapes=[pltpu.VMEM((SUBLANES + rows, E_BRANCH), F32),
                        pltpu.VMEM((SUBLANES + rows, E_BRANCH), F32),
                        pltpu.VMEM((H_C, DH_C, 2 * LANES), F32),
                        pltpu.VMEM((1, LANES), F32)],
        compiler_params=_params("parallel", "arbitrary"),
        name="mlstm",
    )(proj, proj, proj, proj, proj, ifv, irow, frow, wq, wk, bq, bk, fbc, fbr, hn_g, ltri, utri,
      sel, ones, mean)


def _rglru_kernel(x_ref, z_ref, cw_ref, cb_ref, wa_ref, ba_ref, wx_ref, bx_ref, lam_ref,
                  y_ref, xpad, a_s, h_s):
    seq = x_ref.shape[0]
    halo = SUBLANES
    xpad[0:halo, :] = jnp.zeros((halo, E_BRANCH), F32)
    xpad[halo:halo + seq, :] = x_ref[...].astype(F32)
    log_lam = _neg_softplus(-lam_ref[...])

    first = halo - (SHORT_CONV - 1)
    for c in range(seq // ROWS_D):
        r0 = c * ROWS_D
        xc = jnp.broadcast_to(cb_ref[...], (ROWS_D, E_BRANCH))
        for j in range(SHORT_CONV):
            xc = xc + cw_ref[j:j + 1, :] * xpad[r0 + first + j:r0 + first + j + ROWS_D, :]
        xb = xc.astype(BF16)
        r = _sigmoid(_dot(xb, wa_ref[...]) + ba_ref[...])
        i = _sigmoid(_dot(xb, wx_ref[...]) + bx_ref[...])
        log_a = LRU_C * r * log_lam
        a = jnp.exp(log_a)
        u = jnp.sqrt(-jnp.tanh(log_a) * (a * a + 1.0)) * (i * xc)
        a_s[r0:r0 + ROWS_D, :] = a
        h_s[r0:r0 + ROWS_D, :] = u

    sub = lax.broadcasted_iota(jnp.int32, (SUBLANES, E_BRANCH), 0)
    rows_out = 2 * SUBLANES

    def scan_body(g, carry):
        r0 = pl.multiple_of(g * rows_out, rows_out)
        tiles = []
        for part in range(rows_out // SUBLANES):
            rows = pl.ds(r0 + part * SUBLANES, SUBLANES)
            a = a_s[rows, :]
            u = h_s[rows, :]
            for d in (1, 2, 4):
                keep = sub >= d
                u_prev = jnp.where(keep, pltpu.roll(u, d, 0), 0.0)
                a_prev = jnp.where(keep, pltpu.roll(a, d, 0), 1.0)
                u = u + a * u_prev
                a = a * a_prev
            h = u + a * carry
            carry = jnp.broadcast_to(h[SUBLANES - 1:SUBLANES, :], (SUBLANES, E_BRANCH))
            tiles.append(h)
        hs = jnp.concatenate(tiles, axis=0)
        zz = z_ref[pl.ds(r0, rows_out), :].astype(F32)
        y_ref[pl.ds(r0, rows_out), :] = (hs * _silu(zz)).astype(y_ref.dtype)
        return carry

    lax.fori_loop(0, seq // rows_out, scan_body, jnp.zeros((SUBLANES, E_BRANCH), F32))


def _rglru(proj, batch, seq, cw, cb, wa, ba, wx, bx, lam):
    return pl.pallas_call(
        _rglru_kernel,
        grid=(batch,),
        in_specs=[pl.BlockSpec((seq, E_BRANCH), lambda b: (b, COL_D_X)),
                  pl.BlockSpec((seq, E_BRANCH), lambda b: (b, COL_D_Z)),
                  _const_spec((SHORT_CONV, E_BRANCH)), _const_spec((1, E_BRANCH)),
                  _const_spec((E_BRANCH, E_BRANCH)), _const_spec((1, E_BRANCH)),
                  _const_spec((E_BRANCH, E_BRANCH)), _const_spec((1, E_BRANCH)),
                  _const_spec((1, E_BRANCH))],
        out_specs=pl.BlockSpec((seq, E_BRANCH), lambda b: (b, 0)),
        out_shape=jax.ShapeDtypeStruct((batch * seq, E_BRANCH), BF16),
        scratch_shapes=[pltpu.VMEM((SUBLANES + seq, E_BRANCH), F32),
                        pltpu.VMEM((seq, E_BRANCH), F32),
                        pltpu.VMEM((seq, E_BRANCH), F32)],
        compiler_params=_params("parallel"),
        name="rglru",
    )(proj, proj, cw, cb, wa, ba, wx, bx, lam)


def _memkv_kernel(mem_ref, g_ref, w_ref, kv_ref):
    xf = mem_ref[...]
    ms = jnp.mean(xf * xf, axis=-1, keepdims=True)
    mn = (xf * lax.rsqrt(ms + EPS) * g_ref[...]).astype(BF16)
    kv_ref[...] = _dot(mn, w_ref[...]).astype(kv_ref.dtype)


def _memkv(mem2, batch, n_mem, g, w):
    return pl.pallas_call(
        _memkv_kernel,
        grid=(batch,),
        in_specs=[pl.BlockSpec((n_mem, D_MODEL), lambda b: (b, 0)),
                  _const_spec((1, D_MODEL)),
                  _const_spec((D_MODEL, 2 * E_BRANCH))],
        out_specs=pl.BlockSpec((n_mem, 2 * E_BRANCH), lambda b: (b, 0)),
        out_shape=jax.ShapeDtypeStruct((batch * n_mem, 2 * E_BRANCH), BF16),
        compiler_params=_params("parallel"),
        name="memkv",
    )(mem2, g, w)


def _memattn_kernel(q_ref, z_ref, kv_ref, y_ref):
    scale = DH_M ** -0.5
    for h in range(H_M):
        sl = slice(h * LANES, (h + 1) * LANES)
        kh = kv_ref[:, sl]
        vh = kv_ref[:, E_BRANCH + h * LANES:E_BRANCH + (h + 1) * LANES]
        s = _dot_nt(q_ref[:, sl], kh) * scale
        p = jnp.exp(s - jnp.max(s, axis=-1, keepdims=True))
        o = _dot(p.astype(BF16), vh) / jnp.sum(p, axis=-1, keepdims=True)
        y_ref[:, sl] = (o * _silu(z_ref[:, sl].astype(F32))).astype(y_ref.dtype)


def _memattn(proj, kv, batch, seq, n_mem):
    nq = seq // TQ_MEM
    return pl.pallas_call(
        _memattn_kernel,
        grid=(batch, nq),
        in_specs=[pl.BlockSpec((TQ_MEM, E_BRANCH), lambda b, q: (b * nq + q, COL_M_Q)),
                  pl.BlockSpec((TQ_MEM, E_BRANCH), lambda b, q: (b * nq + q, COL_M_Z)),
                  pl.BlockSpec((n_mem, 2 * E_BRANCH), lambda b, q: (b, 0))],
        out_specs=pl.BlockSpec((TQ_MEM, E_BRANCH), lambda b, q: (b * nq + q, 0)),
        out_shape=jax.ShapeDtypeStruct((batch * seq, E_BRANCH), BF16),
        compiler_params=_params("parallel", "parallel"),
        name="memattn",
    )(proj, proj, kv)


def _merge_kernel(x_ref, h_ref, ya_ref, yb_ref, yc_ref, yd_ref, ym_ref, wg_ref, bg_ref,
                  wup_ref, wout_ref, fg_ref, o_ref, *, final_norm):
    hb = h_ref[...]
    merged = None
    for n, y_ref in enumerate((ya_ref, yb_ref, yc_ref, yd_ref, ym_ref)):
        cols = slice(n * D_MODEL, (n + 1) * D_MODEL)
        gate = _sigmoid(_dot(hb, wg_ref[:, cols]) + bg_ref[:, cols])
        term = gate * _dot(y_ref[...], wup_ref[n])
        merged = term if merged is None else merged + term
    out = x_ref[...] + _dot(merged.astype(BF16), wout_ref[...])
    if final_norm:
        ms = jnp.mean(out * out, axis=-1, keepdims=True)
        out = out * lax.rsqrt(ms + EPS) * fg_ref[...]
    o_ref[...] = out


def _merge(x2, h, ys, wg, bg, wup, wout, fg, final_norm):
    t = x2.shape[0]
    row = lambda width: pl.BlockSpec((TM_MERGE, width), lambda m: (m, 0))
    return pl.pallas_call(
        functools.partial(_merge_kernel, final_norm=final_norm),
        grid=(t // TM_MERGE,),
        in_specs=[row(D_MODEL), row(D_MODEL)] + [row(E_BRANCH)] * N_BRANCH + [
            _const_spec((D_MODEL, N_BRANCH * D_MODEL)), _const_spec((1, N_BRANCH * D_MODEL)),
            _const_spec((N_BRANCH, E_BRANCH, D_MODEL)), _const_spec((D_MODEL, D_MODEL)),
            _const_spec((1, D_MODEL))],
        out_specs=row(D_MODEL),
        out_shape=jax.ShapeDtypeStruct((t, D_MODEL), F32),
        compiler_params=_params("parallel"),
        name="merge_final" if final_norm else "merge",
    )(x2, h, *ys, wg, bg, wup, wout, fg)


def _block_diag(w):
    nb, bw, _ = w.shape
    eye = jnp.eye(nb, dtype=w.dtype)
    return (eye[:, None, :, None] * w[:, :, None, :]).reshape(nb * bw, nb * bw)


def _layer(x2, mem2, batch, seq, n_mem, norm_g, w_in, b_in, a_conv_w, a_conv_b, a_ln_g, a_ln_b,
           c_conv_w, c_conv_b, c_f_bias, c_hn_g, d_conv_w, d_conv_b, d_wa, d_ba, d_wx, d_bx,
           d_lambda, mem_norm_g, w_mkv, w_up, w_out, final_g, final_norm, consts):
    umat, ltri, utri = consts
    row = lambda v: v.reshape(1, -1)
    n_pre = 10 * E_BRANCH
    n_if = 2 * H_C
    n_gate0 = N_MAIN + n_if
    w_main = jnp.concatenate([w_in[:, :n_pre], w_in[:, n_pre + n_if:n_gate0]], axis=1).astype(BF16)
    b_main = row(jnp.concatenate([b_in[:n_pre], b_in[n_pre + n_if:n_gate0]]))
    wif = jnp.zeros((D_MODEL, IF_WIDTH), F32)
    wif = wif.at[:, 0:H_C].set(w_in[:, n_pre:n_pre + H_C])
    wif = wif.at[:, LANES:LANES + H_C].set(w_in[:, n_pre + H_C:n_pre + n_if]).astype(BF16)
    bif = jnp.zeros((1, IF_WIDTH), F32)
    bif = bif.at[0, 0:H_C].set(b_in[n_pre:n_pre + H_C])
    bif = bif.at[0, LANES:LANES + H_C].set(b_in[n_pre + H_C:n_pre + n_if])

    proj, h, ifv = _inproj(x2, row(norm_g), w_main, b_main, wif, bif)

    y_a = _conv_a(proj, batch, seq, a_conv_w, row(a_conv_b), row(a_ln_g), row(a_ln_b))
    y_b = _sb_attn(proj, batch, seq, umat)

    pad_rows = ((0, 0), (0, SUBLANES - H_C), (0, 0))
    irow = jnp.pad(ifv[:, 0:H_C].reshape(batch, seq, H_C).transpose(0, 2, 1), pad_rows)
    frow = jnp.pad(ifv[:, LANES:LANES + H_C].reshape(batch, seq, H_C).transpose(0, 2, 1), pad_rows)
    fbc = jnp.zeros((1, LANES), F32).at[0, 0:H_C].set(c_f_bias)
    fbr = jnp.zeros((SUBLANES, LANES), F32).at[0:H_C, :].set(
        jnp.broadcast_to(c_f_bias[:, None], (H_C, LANES)))
    y_c = _mlstm(proj, ifv, irow, frow, batch, seq,
                 c_conv_w[:, :E_BRANCH], c_conv_w[:, E_BRANCH:],
                 row(c_conv_b[:E_BRANCH]), row(c_conv_b[E_BRANCH:]),
                 fbc, fbr, row(c_hn_g), ltri, utri)

    y_d = _rglru(proj, batch, seq, d_conv_w, row(d_conv_b),
                 _block_diag(d_wa).astype(BF16), row(d_ba),
                 _block_diag(d_wx).astype(BF16), row(d_bx), row(d_lambda))

    kv = _memkv(mem2, batch, n_mem, row(mem_norm_g), w_mkv.astype(BF16))
    y_m = _memattn(proj, kv, batch, seq, n_mem)

    return _merge(x2, h, (y_a, y_b, y_c, y_d, y_m), w_in[:, n_gate0:].astype(BF16),
                  row(b_in[n_gate0:]), w_up.astype(BF16), w_out.astype(BF16), row(final_g),
                  final_norm)


def kernel(x, mem, norm_g, w_in, b_in, a_conv_w, a_conv_b, a_ln_g, a_ln_b, c_conv_w, c_conv_b,
           c_f_bias, c_hn_g, d_conv_w, d_conv_b, d_wa, d_ba, d_wx, d_bx, d_lambda, mem_norm_g,
           w_mkv, w_up, w_out, final_norm_g):
    batch, seq, _ = x.shape
    n_mem = mem.shape[1]
    depth = w_in.shape[0]
    assert seq % T_CONV_A == 0 and seq % TQ_MEM == 0 and seq % ROWS_D == 0
    assert seq % (MLSTM_CHUNKS * BLK) == 0 and seq % SB_Q == 0
    assert (batch * seq) % TM_IN == 0 and (batch * seq) % TM_MERGE == 0

    tri = jnp.tril(jnp.ones((BLK, BLK), F32))
    umat = -jnp.concatenate([tri, jnp.ones((BLK, LANES), F32)], axis=1)
    umat = jnp.concatenate([umat, umat], axis=0).astype(BF16)
    consts = (umat, tri.astype(BF16), tri.T.astype(BF16))

    x2 = x.reshape(batch * seq, D_MODEL)
    mem2 = mem.reshape(batch * n_mem, D_MODEL)
    for l in range(depth):
        x2 = _layer(x2, mem2, batch, seq, n_mem, norm_g[l], w_in[l], b_in[l], a_conv_w[l],
                    a_conv_b[l], a_ln_g[l], a_ln_b[l], c_conv_w[l], c_conv_b[l], c_f_bias[l],
                    c_hn_g[l], d_conv_w[l], d_conv_b[l], d_wa[l], d_ba[l], d_wx[l], d_bx[l],
                    d_lambda[l], mem_norm_g[l], w_mkv[l], w_up[l], w_out[l], final_norm_g,
                    l == depth - 1, consts)
    return x2.reshape(batch, seq, D_MODEL)
```

```python
import functools

import jax
import jax.numpy as jnp
from jax import lax
from jax.experimental import pallas as pl
from jax.experimental.pallas import tpu as pltpu

F32 = jnp.float32
BF16 = jnp.bfloat16

D_MODEL = 1024
E_BRANCH = 512
N_BRANCH = 5
CONF_WIDTH = 31
H_B = 8
DH_B = E_BRANCH // H_B
H_C = 4
DH_C = E_BRANCH // H_C
SHORT_CONV = 4
NB_D = 8
BW_D = E_BRANCH // NB_D
LRU_C = 8.0
H_M = 4
DH_M = E_BRANCH // H_M
EPS = 1e-6

LOG2_E = 1.4426950408889634
LANES = 128
SUBLANES = 8
VMEM_LIMIT = 56 * 1024 * 1024

COL_A_VAL, COL_A_GLU, COL_A_Z = 0, 1, 2
COL_B_Q, COL_B_K, COL_B_V, COL_B_Z = 3, 4, 5, 6
COL_C_Q, COL_C_K, COL_C_V, COL_C_O, COL_C_Z = 7, 8, 9, 10, 11
COL_D_X, COL_D_Z = 12, 13
COL_M_Q, COL_M_Z = 14, 15
N_MAIN = 16 * E_BRANCH
IF_WIDTH = 2 * LANES

TM_IN, TN_IN = 1024, 2048
T_CONV_A = 512
ROWS_CONV_A = 32
HALO_A = 32
BLK = 128
SB_Q = 256
SB_GROUP = 4
MLSTM_CHUNKS = 4
TQ_MEM = 1024
TM_MERGE = 512
ROWS_D = 256


def _dot(a, b):
    return jnp.dot(a, b, preferred_element_type=F32)


def _dot_nt(a, b):
    return lax.dot_general(a, b, (((1,), (1,)), ((), ())), preferred_element_type=F32)


def _sigmoid(x):
    return 0.5 * jnp.tanh(0.5 * x) + 0.5


def _silu(x):
    return x * _sigmoid(x)


def _softplus(x):
    return jnp.maximum(x, 0.0) + jnp.log(1.0 + jnp.exp2(jnp.abs(x) * (-LOG2_E)))


def _neg_softplus(x):
    return -_softplus(x)


def _split_bf16(x, parts):
    out = []
    r = x
    for _ in range(parts):
        p = r.astype(BF16)
        out.append(p)
        r = r - p.astype(F32)
    return out


def _params(*sem):
    return pltpu.CompilerParams(dimension_semantics=sem, vmem_limit_bytes=VMEM_LIMIT)


def _const_spec(shape):
    nd = len(shape)
    return pl.BlockSpec(shape, lambda *_: (0,) * nd, pipeline_mode=pl.Buffered(1))


def _inproj_kernel(x_ref, g_ref, w_ref, b_ref, wif_ref, bif_ref, proj_ref, h_ref, if_ref, hs_ref):
    @pl.when(pl.program_id(1) == 0)
    def _():
        xf = x_ref[...]
        ms = jnp.mean(xf * xf, axis=-1, keepdims=True)
        hb = (xf * lax.rsqrt(ms + EPS) * g_ref[...]).astype(BF16)
        hs_ref[...] = hb
        h_ref[...] = hb
        if_ref[...] = _dot(hb, wif_ref[...]) + bif_ref[...]

    proj_ref[...] = (_dot(hs_ref[...], w_ref[...]) + b_ref[...]).astype(proj_ref.dtype)


def _inproj(x2, g, w_main, b_main, wif, bif):
    t = x2.shape[0]
    return pl.pallas_call(
        _inproj_kernel,
        grid=(t // TM_IN, N_MAIN // TN_IN),
        in_specs=[
            pl.BlockSpec((TM_IN, D_MODEL), lambda m, n: (m, 0)),
            _const_spec((1, D_MODEL)),
            pl.BlockSpec((D_MODEL, TN_IN), lambda m, n: (0, n)),
            pl.BlockSpec((1, TN_IN), lambda m, n: (0, n)),
            _const_spec((D_MODEL, IF_WIDTH)),
            _const_spec((1, IF_WIDTH)),
        ],
        out_specs=[
            pl.BlockSpec((TM_IN, TN_IN), lambda m, n: (m, n)),
            pl.BlockSpec((TM_IN, D_MODEL), lambda m, n: (m, 0)),
            pl.BlockSpec((TM_IN, IF_WIDTH), lambda m, n: (m, 0)),
        ],
        out_shape=[
            jax.ShapeDtypeStruct((t, N_MAIN), BF16),
            jax.ShapeDtypeStruct((t, D_MODEL), BF16),
            jax.ShapeDtypeStruct((t, IF_WIDTH), F32),
        ],
        scratch_shapes=[pltpu.VMEM((TM_IN, D_MODEL), BF16)],
        compiler_params=_params("parallel", "arbitrary"),
        name="inproj",
    )(x2, g, w_main, b_main, wif, bif)


def _conv_a_kernel(val_ref, glu_ref, z_ref, cw_ref, cb_ref, lg_ref, lb_ref, o_ref, upad, ush):
    s = pl.program_id(1)

    @pl.when(s == 0)
    def _():
        upad[0:HALO_A, :] = jnp.zeros((HALO_A, E_BRANCH), F32)

    @pl.when(s > 0)
    def _():
        upad[0:HALO_A, :] = upad[T_CONV_A:T_CONV_A + HALO_A, :]

    upad[HALO_A:HALO_A + T_CONV_A, :] = val_ref[...].astype(F32) * _sigmoid(glu_ref[...].astype(F32))

    n_rows = HALO_A + T_CONV_A
    for r in range(1, SUBLANES):
        ush[r - 1, SUBLANES:n_rows, :] = upad[SUBLANES - r:n_rows - r, :]

    for c in range(T_CONV_A // ROWS_CONV_A):
        r0 = c * ROWS_CONV_A
        acc = jnp.broadcast_to(cb_ref[...], (ROWS_CONV_A, E_BRANCH))
        for j in range(CONF_WIDTH):
            tiles_back, r = divmod(CONF_WIDTH - 1 - j, SUBLANES)
            start = HALO_A + r0 - SUBLANES * tiles_back
            if r == 0:
                src = upad[start:start + ROWS_CONV_A, :]
            else:
                src = ush[r - 1, start:start + ROWS_CONV_A, :]
            acc = acc + cw_ref[j:j + 1, :] * src
        mu = jnp.mean(acc, axis=-1, keepdims=True)
        d = acc - mu
        var = jnp.mean(d * d, axis=-1, keepdims=True)
        y = d * lax.rsqrt(var + EPS) * lg_ref[...] + lb_ref[...]
        zz = z_ref[r0:r0 + ROWS_CONV_A, :].astype(F32)
        o_ref[r0:r0 + ROWS_CONV_A, :] = (_silu(y) * _silu(zz)).astype(o_ref.dtype)


def _conv_a(proj, batch, seq, cw, cb, lg, lb):
    ns = seq // T_CONV_A

    def col(c):
        return pl.BlockSpec((T_CONV_A, E_BRANCH), lambda b, s: (b * ns + s, c))

    return pl.pallas_call(
        _conv_a_kernel,
        grid=(batch, ns),
        in_specs=[col(COL_A_VAL), col(COL_A_GLU), col(COL_A_Z),
                  _const_spec((CONF_WIDTH, E_BRANCH)), _const_spec((1, E_BRANCH)),
                  _const_spec((1, E_BRANCH)), _const_spec((1, E_BRANCH))],
        out_specs=pl.BlockSpec((T_CONV_A, E_BRANCH), lambda b, s: (b * ns + s, 0)),
        out_shape=jax.ShapeDtypeStruct((batch * seq, E_BRANCH), BF16),
        scratch_shapes=[pltpu.VMEM((HALO_A + T_CONV_A, E_BRANCH), F32),
                        pltpu.VMEM((SUBLANES - 1, HALO_A + T_CONV_A, E_BRANCH), F32)],
        compiler_params=_params("parallel", "arbitrary"),
        name="conv_a",
    )(proj, proj, proj, cw, cb, lg, lb)


def _sb_kernel(q_ref, k_ref, v_ref, zg_ref, u2_ref, o_ref,
               k2_s, v2_s, q_s, z_s, hl_s, w_s, acc_s, r_s):
    qi = pl.program_id(1)
    n_pair = H_B // 2
    n_kb = k_ref.shape[0] // BLK
    kb_per_q = SB_Q // BLK
    lane = lax.broadcasted_iota(jnp.int32, (SB_Q, LANES), 1)
    row = lax.broadcasted_iota(jnp.int32, (SB_Q, LANES), 0)
    strict = [(j * BLK + lane) < row for j in range(kb_per_q)]

    @pl.when(qi == 0)
    def _():
        lane_e = lax.broadcasted_iota(jnp.int32, (BLK, E_BRANCH), 1)
        first_head = (lane_e % LANES) < DH_B

        def fill(kb, carry):
            src = pl.multiple_of(kb * BLK, BLK)
            dst = pl.multiple_of(kb * 2 * BLK, 2 * BLK)
            for src_ref, dst_ref in ((k_ref, k2_s), (v_ref, v2_s)):
                blk = src_ref[pl.ds(src, BLK), :].astype(F32)
                dst_ref[pl.ds(dst, BLK), :] = jnp.where(first_head, blk, 0.0).astype(BF16)
                dst_ref[pl.ds(dst + BLK, BLK), :] = jnp.where(first_head, 0.0, blk).astype(BF16)
            return carry

        lax.fori_loop(0, n_kb, fill, 0)

    q_s[...] = (q_ref[...].astype(F32) * (DH_B ** -0.5)).astype(BF16)

    def multi_step(kb_top, n_blk, diag):
        base = pl.multiple_of((kb_top - (n_blk - 1)) * 2 * BLK, 2 * BLK)
        for k in range(n_blk):
            rows2 = pl.ds(base + (n_blk - 1 - k) * 2 * BLK, 2 * BLK)
            for p in range(n_pair):
                lanes = slice(p * LANES, (p + 1) * LANES)
                z2 = _dot_nt(q_s[:, lanes], k2_s[rows2, lanes])
                nlk = _softplus(z2)
                if diag:
                    mask = strict[n_blk - 1 - k]
                    nlk = jnp.where(jnp.concatenate([mask, mask], axis=1), nlk, 0.0)
                z_s[k, p] = z2
                hi, lo = _split_bf16(nlk, 2)
                for hh in range(2):
                    half = slice(hh * LANES, (hh + 1) * LANES)
                    hl_s[k, 2 * p + hh] = jnp.concatenate([hi[:, half], lo[:, half]], axis=1)
        for h in range(H_B):
            p, hh = divmod(h, 2)
            half = slice(hh * LANES, (hh + 1) * LANES)
            r = None if diag else r_s[h]
            for k in range(n_blk):
                cs = _dot(hl_s[k, h], u2_ref[...])
                cum = cs[:, :LANES]
                tot = cs[:, LANES:]
                log_w = z_s[k, p, :, half] + cum
                if r is not None:
                    log_w = log_w + r
                w = jnp.exp(log_w)
                if diag:
                    w = jnp.where(strict[n_blk - 1 - k], w, 0.0)
                r = tot if r is None else r + tot
                col = (n_blk - 1 - k) * 2 * LANES + hh * LANES
                w_s[p, :, col:col + LANES] = w.astype(BF16)
            r_s[h] = r
        for p in range(n_pair):
            lanes = slice(p * LANES, (p + 1) * LANES)
            pv = _dot(w_s[p, :, :n_blk * 2 * LANES], v2_s[pl.ds(base, n_blk * 2 * BLK), lanes])
            if diag:
                acc_s[p] = pv
            else:
                acc_s[p] += pv

    n_off = qi * kb_per_q
    multi_step(n_off + kb_per_q - 1, kb_per_q, True)

    n_full = n_off // SB_GROUP

    def body(i, carry):
        multi_step(n_off - 1 - SB_GROUP * i, SB_GROUP, False)
        return carry

    lax.fori_loop(0, n_full, body, 0)

    top = n_off - 1 - SB_GROUP * n_full
    rem = n_off - SB_GROUP * n_full
    size = SB_GROUP // 2
    while size >= 1:
        take = (rem & size) != 0

        @pl.when(take)
        def _(top=top, size=size):
            multi_step(top, size, False)

        top = top - jnp.where(take, size, 0)
        size //= 2

    for p in range(n_pair):
        lanes = slice(p * LANES, (p + 1) * LANES)
        zz = zg_ref[:, lanes].astype(F32)
        o_ref[:, lanes] = (acc_s[p] * _silu(zz)).astype(o_ref.dtype)


def _sb_attn(proj, batch, seq, umat):
    nq = seq // SB_Q
    return pl.pallas_call(
        _sb_kernel,
        grid=(batch, nq),
        in_specs=[
            pl.BlockSpec((SB_Q, E_BRANCH), lambda b, q: (b * nq + q, COL_B_Q)),
            pl.BlockSpec((seq, E_BRANCH), lambda b, q: (b, COL_B_K)),
            pl.BlockSpec((seq, E_BRANCH), lambda b, q: (b, COL_B_V)),
            pl.BlockSpec((SB_Q, E_BRANCH), lambda b, q: (b * nq + q, COL_B_Z)),
            _const_spec((2 * BLK, 2 * LANES)),
        ],
        out_specs=pl.BlockSpec((SB_Q, E_BRANCH), lambda b, q: (b * nq + q, 0)),
        out_shape=jax.ShapeDtypeStruct((batch * seq, E_BRANCH), BF16),
        scratch_shapes=[pltpu.VMEM((2 * seq, E_BRANCH), BF16),
                        pltpu.VMEM((2 * seq, E_BRANCH), BF16),
                        pltpu.VMEM((SB_Q, E_BRANCH), BF16),
                        pltpu.VMEM((SB_GROUP, H_B // 2, SB_Q, 2 * LANES), F32),
                        pltpu.VMEM((SB_GROUP, H_B, SB_Q, 2 * LANES), BF16),
                        pltpu.VMEM((H_B // 2, SB_Q, SB_GROUP * 2 * LANES), BF16),
                        pltpu.VMEM((H_B // 2, SB_Q, LANES), F32),
                        pltpu.VMEM((H_B, SB_Q, LANES), F32)],
        compiler_params=_params("parallel", "arbitrary"),
        name="sb_attn",
    )(proj, proj, proj, proj, umat)


def _mlstm_kernel(q_ref, k_ref, v_ref, og_ref, z_ref, ifc_ref, ir_ref, fr_ref,
                  wq_ref, wk_ref, bq_ref, bk_ref, fbc_ref, fbr_ref, g_ref, lt_ref, ut_ref,
                  sel_ref, ones_ref, mean_ref, y_ref, qpad, kpad, ct_ref, m_ref):
    step = pl.program_id(1)
    halo = SUBLANES
    rows = MLSTM_CHUNKS * BLK

    @pl.when(step == 0)
    def _():
        qpad[0:halo, :] = jnp.zeros((halo, E_BRANCH), F32)
        kpad[0:halo, :] = jnp.zeros((halo, E_BRANCH), F32)
        ct_ref[...] = jnp.zeros(ct_ref.shape, F32)
        m_ref[...] = jnp.zeros(m_ref.shape, F32)

    @pl.when(step > 0)
    def _():
        qpad[0:halo, :] = qpad[rows:rows + halo, :]
        kpad[0:halo, :] = kpad[rows:rows + halo, :]

    qpad[halo:halo + rows, :] = q_ref[...].astype(F32)
    kpad[halo:halo + rows, :] = k_ref[...].astype(F32)

    lane = lax.broadcasted_iota(jnp.int32, (BLK, LANES), 1)
    row = lax.broadcasted_iota(jnp.int32, (BLK, LANES), 0)
    causal = lane <= row
    first = halo - (SHORT_CONV - 1)
    chunks = range(MLSTM_CHUNKS)
    heads = range(H_C)
    sls = [slice(h * LANES, (h + 1) * LANES) for h in heads]

    def lane_sum(x, ones_ref):
        return _dot(jnp.concatenate(_split_bf16(x, 2), axis=1), ones_ref[...])

    qc, kc, g_row, spread = [], [], [], []
    m_prev = m_ref[...]
    for c in chunks:
        r0 = c * BLK
        q_acc = jnp.broadcast_to(bq_ref[...], (BLK, E_BRANCH))
        k_acc = jnp.broadcast_to(bk_ref[...], (BLK, E_BRANCH))
        for j in range(SHORT_CONV):
            q_acc = q_acc + wq_ref[j:j + 1, :] * qpad[r0 + first + j:r0 + first + j + BLK, :]
            k_acc = k_acc + wk_ref[j:j + 1, :] * kpad[r0 + first + j:r0 + first + j + BLK, :]
        qc.append(_silu(q_acc).astype(BF16))
        kc.append(_silu(k_acc) * (DH_C ** -0.5))

        ic = ifc_ref[r0:r0 + BLK, 0:LANES]
        fc = _neg_softplus(-(ifc_ref[r0:r0 + BLK, LANES:2 * LANES] + fbc_ref[...]))
        bc = sum(_dot(lt_ref[...], part) for part in _split_bf16(fc, 3))
        fr = _neg_softplus(-(fr_ref[0, :, r0:r0 + BLK] + fbr_ref[...]))
        br = sum(_dot(part, ut_ref[...]) for part in _split_bf16(fr, 3))
        g_row.append(ir_ref[0, :, r0:r0 + BLK] - br)

        run = ic - bc
        d = 1
        while d < BLK:
            shifted = jnp.concatenate([jnp.full((d, LANES), -jnp.inf, F32), run[:BLK - d, :]], axis=0)
            run = jnp.maximum(run, shifted)
            d *= 2
        inter = bc + m_prev
        mt = jnp.maximum(inter, bc + run)

        b_last = bc[BLK - 1:BLK, :]
        wlog = b_last - bc + ic
        m_new = jnp.maximum(b_last + m_prev, jnp.max(wlog, axis=0, keepdims=True))
        decay = jnp.exp(b_last + m_prev - m_new)
        stack = jnp.concatenate([bc - mt, jnp.exp(inter - mt), jnp.exp(-mt), jnp.exp(wlog - m_new),
                                 jnp.broadcast_to(decay, (SUBLANES, LANES))], axis=0)
        spread.append(_dot(jnp.concatenate(_split_bf16(stack, 2), axis=1), sel_ref[...]))
        m_prev = m_new
    m_ref[...] = m_prev

    def spread_of(c, h, part):
        rows_ = SUBLANES if part == 4 else BLK
        return spread[c][part * BLK:part * BLK + rows_, sls[h]]

    s_qk, qct = [], []
    for c in chunks:
        r0 = c * BLK
        s_qk.append([_dot_nt(qc[c][:, sls[h]], kc[c][:, sls[h]].astype(BF16)) for h in heads])
        qct.append([_dot(qc[c][:, sls[h]], ct_ref[h].astype(BF16)) for h in heads])
        for h in heads:
            w_st = spread_of(c, h, 3)
            vw = jnp.concatenate([v_ref[r0:r0 + BLK, sls[h]].astype(F32) * w_st, w_st], axis=1)
            k_t = kc[c][:, sls[h]].T.astype(BF16)
            dec = jnp.broadcast_to(spread_of(c, h, 4)[0:1, :], (DH_C, LANES))
            ct_ref[h] = (jnp.concatenate([dec, dec], axis=1) * ct_ref[h]
                         + _dot(k_t, vw.astype(BF16)))

    units = [(c, h) for c in chunks for h in heads]
    sc = {}
    for c, h in units:
        dlog_mt = jnp.where(causal, spread_of(c, h, 0) + g_row[c][h:h + 1, :], -jnp.inf)
        sc[c, h] = s_qk[c][h] * jnp.exp(dlog_mt)
    num_intra = {(c, h): _dot(sc[c, h].astype(BF16), v_ref[c * BLK:(c + 1) * BLK, sls[h]])
                 for c, h in units}
    den_intra = {u: lane_sum(sc[u], ones_ref) for u in units}
    hh = {}
    for c, h in units:
        w_inter = spread_of(c, h, 1)
        num = num_intra[c, h] + w_inter * qct[c][h][:, :LANES]
        den = den_intra[c, h] + w_inter * qct[c][h][:, LANES:]
        hh[c, h] = num / jnp.maximum(jnp.abs(den), spread_of(c, h, 2))
    mu = {u: lane_sum(hh[u], mean_ref) for u in units}
    dd = {u: hh[u] - mu[u] for u in units}
    var = {u: lane_sum(dd[u] * dd[u], mean_ref) for u in units}
    for c, h in units:
        rws = slice(c * BLK, (c + 1) * BLK)
        hn = dd[c, h] * lax.rsqrt(var[c, h] + EPS) * g_ref[:, sls[h]]
        y = (hn * _sigmoid(og_ref[rws, sls[h]].astype(F32))
             * _silu(z_ref[rws, sls[h]].astype(F32)))
        y_ref[rws, sls[h]] = y.astype(y_ref.dtype)


def _mlstm(proj, ifv, irow, frow, batch, seq, wq, wk, bq, bk, fbc, fbr, hn_g, ltri, utri):
    rows = MLSTM_CHUNKS * BLK
    nc = seq // rows
    k_idx = jnp.arange(2 * LANES)[:, None] % LANES
    sel = (k_idx == jnp.arange(E_BRANCH)[None, :] // LANES).astype(BF16)
    ones = jnp.ones((2 * LANES, LANES), BF16)
    mean = jnp.full((2 * LANES, LANES), 1.0 / DH_C, BF16)

    def col(c):
        return pl.BlockSpec((rows, E_BRANCH), lambda b, s: (b * nc + s, c))

    row_spec = pl.BlockSpec((1, SUBLANES, rows), lambda b, s: (b, 0, s))
    return pl.pallas_call(
        _mlstm_kernel,
        grid=(batch, nc),
        in_specs=[col(COL_C_Q), col(COL_C_K), col(COL_C_V), col(COL_C_O), col(COL_C_Z),
                  pl.BlockSpec((rows, IF_WIDTH), lambda b, s: (b * nc + s, 0)),
                  row_spec, row_spec,
                  _const_spec((SHORT_CONV, E_BRANCH)), _const_spec((SHORT_CONV, E_BRANCH)),
                  _const_spec((1, E_BRANCH)), _const_spec((1, E_BRANCH)),
                  _const_spec((1, LANES)), _const_spec((SUBLANES, LANES)),
                  _const_spec((1, E_BRANCH)),
                  _const_spec((BLK, BLK)), _const_spec((BLK, BLK)),
                  _const_spec((2 * LANES, E_BRANCH)), _const_spec((2 * LANES, LANES)),
                  _const_spec((2 * LANES, LANES))],
        out_specs=pl.BlockSpec((rows, E_BRANCH), lambda b, s: (b * nc + s, 0)),
        out_shape=jax.ShapeDtypeStruct((batch * seq, E_BRANCH), BF16),
        scratch_shapes=[pltpu.VMEM((SUBLANES + rows, E_BRANCH), F32),
                        pltpu.VMEM((SUBLANES + rows, E_BRANCH), F32),
                        pltpu.VMEM((H_C, DH_C, 2 * LANES), F32),
                        pltpu.VMEM((1, LANES), F32)],
        compiler_params=_params("parallel", "arbitrary"),
        name="mlstm",
    )(proj, proj, proj, proj, proj, ifv, irow, frow, wq, wk, bq, bk, fbc, fbr, hn_g, ltri, utri,
      sel, ones, mean)


def _rglru_kernel(x_ref, z_ref, cw_ref, cb_ref, wa_ref, ba_ref, wx_ref, bx_ref, lam_ref,
                  y_ref, xpad, a_s, h_s):
    seq = x_ref.shape[0]
    halo = SUBLANES
    xpad[0:halo, :] = jnp.zeros((halo, E_BRANCH), F32)
    xpad[halo:halo + seq, :] = x_ref[...].astype(F32)
    log_lam = _neg_softplus(-lam_ref[...])

    first = halo - (SHORT_CONV - 1)
    for c in range(seq // ROWS_D):
        r0 = c * ROWS_D
        xc = jnp.broadcast_to(cb_ref[...], (ROWS_D, E_BRANCH))
        for j in range(SHORT_CONV):
            xc = xc + cw_ref[j:j + 1, :] * xpad[r0 + first + j:r0 + first + j + ROWS_D, :]
        xb = xc.astype(BF16)
        r = _sigmoid(_dot(xb, wa_ref[...]) + ba_ref[...])
        i = _sigmoid(_dot(xb, wx_ref[...]) + bx_ref[...])
        log_a = LRU_C * r * log_lam
        a = jnp.exp(log_a)
        u = jnp.sqrt(-jnp.tanh(log_a) * (a * a + 1.0)) * (i * xc)
        a_s[r0:r0 + ROWS_D, :] = a
        h_s[r0:r0 + ROWS_D, :] = u

    sub = lax.broadcasted_iota(jnp.int32, (SUBLANES, E_BRANCH), 0)
    rows_out = 2 * SUBLANES

    def scan_body(g, carry):
        r0 = pl.multiple_of(g * rows_out, rows_out)
        tiles = []
        for part in range(rows_out // SUBLANES):
            rows = pl.ds(r0 + part * SUBLANES, SUBLANES)
            a = a_s[rows, :]
            u = h_s[rows, :]
            for d in (1, 2, 4):
                keep = sub >= d
                u_prev = jnp.where(keep, pltpu.roll(u, d, 0), 0.0)
                a_prev = jnp.where(keep, pltpu.roll(a, d, 0), 1.0)
                u = u + a * u_prev
                a = a * a_prev
            h = u + a * carry
            carry = jnp.broadcast_to(h[SUBLANES - 1:SUBLANES, :], (SUBLANES, E_BRANCH))
            tiles.append(h)
        hs = jnp.concatenate(tiles, axis=0)
        zz = z_ref[pl.ds(r0, rows_out), :].astype(F32)
        y_ref[pl.ds(r0, rows_out), :] = (hs * _silu(zz)).astype(y_ref.dtype)
        return carry

    lax.fori_loop(0, seq // rows_out, scan_body, jnp.zeros((SUBLANES, E_BRANCH), F32))


def _rglru(proj, batch, seq, cw, cb, wa, ba, wx, bx, lam):
    return pl.pallas_call(
        _rglru_kernel,
        grid=(batch,),
        in_specs=[pl.BlockSpec((seq, E_BRANCH), lambda b: (b, COL_D_X)),
                  pl.BlockSpec((seq, E_BRANCH), lambda b: (b, COL_D_Z)),
                  _const_spec((SHORT_CONV, E_BRANCH)), _const_spec((1, E_BRANCH)),
                  _const_spec((E_BRANCH, E_BRANCH)), _const_spec((1, E_BRANCH)),
                  _const_spec((E_BRANCH, E_BRANCH)), _const_spec((1, E_BRANCH)),
                  _const_spec((1, E_BRANCH))],
        out_specs=pl.BlockSpec((seq, E_BRANCH), lambda b: (b, 0)),
        out_shape=jax.ShapeDtypeStruct((batch * seq, E_BRANCH), BF16),
        scratch_shapes=[pltpu.VMEM((SUBLANES + seq, E_BRANCH), F32),
                        pltpu.VMEM((seq, E_BRANCH), F32),
                        pltpu.VMEM((seq, E_BRANCH), F32)],
        compiler_params=_params("parallel"),
        name="rglru",
    )(proj, proj, cw, cb, wa, ba, wx, bx, lam)


def _memkv_kernel(mem_ref, g_ref, w_ref, kv_ref):
    xf = mem_ref[...]
    ms = jnp.mean(xf * xf, axis=-1, keepdims=True)
    mn = (xf * lax.rsqrt(ms + EPS) * g_ref[...]).astype(BF16)
    kv_ref[...] = _dot(mn, w_ref[...]).astype(kv_ref.dtype)


def _memkv(mem2, batch, n_mem, g, w):
    return pl.pallas_call(
        _memkv_kernel,
        grid=(batch,),
        in_specs=[pl.BlockSpec((n_mem, D_MODEL), lambda b: (b, 0)),
                  _const_spec((1, D_MODEL)),
                  _const_spec((D_MODEL, 2 * E_BRANCH))],
        out_specs=pl.BlockSpec((n_mem, 2 * E_BRANCH), lambda b: (b, 0)),
        out_shape=jax.ShapeDtypeStruct((batch * n_mem, 2 * E_BRANCH), BF16),
        compiler_params=_params("parallel"),
        name="memkv",
    )(mem2, g, w)


def _memattn_kernel(q_ref, z_ref, kv_ref, y_ref):
    scale = DH_M ** -0.5
    for h in range(H_M):
        sl = slice(h * LANES, (h + 1) * LANES)
        kh = kv_ref[:, sl]
        vh = kv_ref[:, E_BRANCH + h * LANES:E_BRANCH + (h + 1) * LANES]
        s = _dot_nt(q_ref[:, sl], kh) * scale
        p = jnp.exp(s - jnp.max(s, axis=-1, keepdims=True))
        o = _dot(p.astype(BF16), vh) / jnp.sum(p, axis=-1, keepdims=True)
        y_ref[:, sl] = (o * _silu(z_ref[:, sl].astype(F32))).astype(y_ref.dtype)


def _memattn(proj, kv, batch, seq, n_mem):
    nq = seq // TQ_MEM
    return pl.pallas_call(
        _memattn_kernel,
        grid=(batch, nq),
        in_specs=[pl.BlockSpec((TQ_MEM, E_BRANCH), lambda b, q: (b * nq + q, COL_M_Q)),
                  pl.BlockSpec((TQ_MEM, E_BRANCH), lambda b, q: (b * nq + q, COL_M_Z)),
                  pl.BlockSpec((n_mem, 2 * E_BRANCH), lambda b, q: (b, 0))],
        out_specs=pl.BlockSpec((TQ_MEM, E_BRANCH), lambda b, q: (b * nq + q, 0)),
        out_shape=jax.ShapeDtypeStruct((batch * seq, E_BRANCH), BF16),
        compiler_params=_params("parallel", "parallel"),
        name="memattn",
    )(proj, proj, kv)


def _merge_kernel(x_ref, h_ref, ya_ref, yb_ref, yc_ref, yd_ref, ym_ref, wg_ref, bg_ref,
                  wup_ref, wout_ref, fg_ref, o_ref, *, final_norm):
    hb = h_ref[...]
    merged = None
    for n, y_ref in enumerate((ya_ref, yb_ref, yc_ref, yd_ref, ym_ref)):
        cols = slice(n * D_MODEL, (n + 1) * D_MODEL)
        gate = _sigmoid(_dot(hb, wg_ref[:, cols]) + bg_ref[:, cols])
        term = gate * _dot(y_ref[...], wup_ref[n])
        merged = term if merged is None else merged + term
    out = x_ref[...] + _dot(merged.astype(BF16), wout_ref[...])
    if final_norm:
        ms = jnp.mean(out * out, axis=-1, keepdims=True)
        out = out * lax.rsqrt(ms + EPS) * fg_ref[...]
    o_ref[...] = out


def _merge(x2, h, ys, wg, bg, wup, wout, fg, final_norm):
    t = x2.shape[0]
    row = lambda width: pl.BlockSpec((TM_MERGE, width), lambda m: (m, 0))
    return pl.pallas_call(
        functools.partial(_merge_kernel, final_norm=final_norm),
        grid=(t // TM_MERGE,),
        in_specs=[row(D_MODEL), row(D_MODEL)] + [row(E_BRANCH)] * N_BRANCH + [
            _const_spec((D_MODEL, N_BRANCH * D_MODEL)), _const_spec((1, N_BRANCH * D_MODEL)),
            _const_spec((N_BRANCH, E_BRANCH, D_MODEL)), _const_spec((D_MODEL, D_MODEL)),
            _const_spec((1, D_MODEL))],
        out_specs=row(D_MODEL),
        out_shape=jax.ShapeDtypeStruct((t, D_MODEL), F32),
        compiler_params=_params("parallel"),
        name="merge_final" if final_norm else "merge",
    )(x2, h, *ys, wg, bg, wup, wout, fg)


def _block_diag(w):
    nb, bw, _ = w.shape
    eye = jnp.eye(nb, dtype=w.dtype)
    return (eye[:, None, :, None] * w[:, :, None, :]).reshape(nb * bw, nb * bw)


def _layer(x2, mem2, batch, seq, n_mem, norm_g, w_in, b_in, a_conv_w, a_conv_b, a_ln_g, a_ln_b,
           c_conv_w, c_conv_b, c_f_bias, c_hn_g, d_conv_w, d_conv_b, d_wa, d_ba, d_wx, d_bx,
           d_lambda, mem_norm_g, w_mkv, w_up, w_out, final_g, final_norm, consts):
    umat, ltri, utri = consts
    row = lambda v: v.reshape(1, -1)
    n_pre = 10 * E_BRANCH
    n_if = 2 * H_C
    n_gate0 = N_MAIN + n_if
    w_main = jnp.concatenate([w_in[:, :n_pre], w_in[:, n_pre + n_if:n_gate0]], axis=1).astype(BF16)
    b_main = row(jnp.concatenate([b_in[:n_pre], b_in[n_pre + n_if:n_gate0]]))
    wif = jnp.zeros((D_MODEL, IF_WIDTH), F32)
    wif = wif.at[:, 0:H_C].set(w_in[:, n_pre:n_pre + H_C])
    wif = wif.at[:, LANES:LANES + H_C].set(w_in[:, n_pre + H_C:n_pre + n_if]).astype(BF16)
    bif = jnp.zeros((1, IF_WIDTH), F32)
    bif = bif.at[0, 0:H_C].set(b_in[n_pre:n_pre + H_C])
    bif = bif.at[0, LANES:LANES + H_C].set(b_in[n_pre + H_C:n_pre + n_if])

    proj, h, ifv = _inproj(x2, row(norm_g), w_main, b_main, wif, bif)

    y_a = _conv_a(proj, batch, seq, a_conv_w, row(a_conv_b), row(a_ln_g), row(a_ln_b))
    y_b = _sb_attn(proj, batch, seq, umat)

    pad_rows = ((0, 0), (0, SUBLANES - H_C), (0, 0))
    irow = jnp.pad(ifv[:, 0:H_C].reshape(batch, seq, H_C).transpose(0, 2, 1), pad_rows)
    frow = jnp.pad(ifv[:, LANES:LANES + H_C].reshape(batch, seq, H_C).transpose(0, 2, 1), pad_rows)
    fbc = jnp.zeros((1, LANES), F32).at[0, 0:H_C].set(c_f_bias)
    fbr = jnp.zeros((SUBLANES, LANES), F32).at[0:H_C, :].set(
        jnp.broadcast_to(c_f_bias[:, None], (H_C, LANES)))
    y_c = _mlstm(proj, ifv, irow, frow, batch, seq,
                 c_conv_w[:, :E_BRANCH], c_conv_w[:, E_BRANCH:],
                 row(c_conv_b[:E_BRANCH]), row(c_conv_b[E_BRANCH:]),
                 fbc, fbr, row(c_hn_g), ltri, utri)

    y_d = _rglru(proj, batch, seq, d_conv_w, row(d_conv_b),
                 _block_diag(d_wa).astype(BF16), row(d_ba),
                 _block_diag(d_wx).astype(BF16), row(d_bx), row(d_lambda))

    kv = _memkv(mem2, batch, n_mem, row(mem_norm_g), w_mkv.astype(BF16))
    y_m = _memattn(proj, kv, batch, seq, n_mem)

    return _merge(x2, h, (y_a, y_b, y_c, y_d, y_m), w_in[:, n_gate0:].astype(BF16),
                  row(b_in[n_gate0:]), w_up.astype(BF16), w_out.astype(BF16), row(final_g),
                  final_norm)


def kernel(x, mem, norm_g, w_in, b_in, a_conv_w, a_conv_b, a_ln_g, a_ln_b, c_conv_w, c_conv_b,
           c_f_bias, c_hn_g, d_conv_w, d_conv_b, d_wa, d_ba, d_wx, d_bx, d_lambda, mem_norm_g,
           w_mkv, w_up, w_out, final_norm_g):
    batch, seq, _ = x.shape
    n_mem = mem.shape[1]
    depth = w_in.shape[0]
    assert seq % T_CONV_A == 0 and seq % TQ_MEM == 0 and seq % ROWS_D == 0
    assert seq % (MLSTM_CHUNKS * BLK) == 0 and seq % SB_Q == 0
    assert (batch * seq) % TM_IN == 0 and (batch * seq) % TM_MERGE == 0

    tri = jnp.tril(jnp.ones((BLK, BLK), F32))
    umat = -jnp.concatenate([tri, jnp.ones((BLK, LANES), F32)], axis=1)
    umat = jnp.concatenate([umat, umat], axis=0).astype(BF16)
    consts = (umat, tri.astype(BF16), tri.T.astype(BF16))

    x2 = x.reshape(batch * seq, D_MODEL)
    mem2 = mem.reshape(batch * n_mem, D_MODEL)
    for l in range(depth):
        x2 = _layer(x2, mem2, batch, seq, n_mem, norm_g[l], w_in[l], b_in[l], a_conv_w[l],
                    a_conv_b[l], a_ln_g[l], a_ln_b[l], c_conv_w[l], c_conv_b[l], c_f_bias[l],
                    c_hn_g[l], d_conv_w[l], d_conv_b[l], d_wa[l], d_ba[l], d_wx[l], d_bx[l],
                    d_lambda[l], mem_norm_g[l], w_mkv[l], w_up[l], w_out[l], final_norm_g,
                    l == depth - 1, consts)
    return x2.reshape(batch, seq, D_MODEL)
```
